```python
import math
import jax, jax.numpy as jnp
from jax import lax
import numpy as np

D_MODEL = 1024
BATCH = 8
SEQ = 4096
DEPTH = 2
DEC_BATCH = 32
DEC_SEQ = 4
PAST_LEN = 16384
PAGE_SIZE = 128

HEAD_DIM = 64
H_A = 8
H_B = 8
W_A = H_A * HEAD_DIM
W_B = H_B * HEAD_DIM
MOBA_BLOCK = 256
MOBA_TOPK = 3
PAGES_PER_BLOCK = MOBA_BLOCK // PAGE_SIZE
Q_BLOCK = 128
ROPE_THETA = 10000.0
D_FF = 3584
N_EXPERTS = 8
TOP_K = 2
N_DENSE = (DEPTH + 1) // 2
N_MOE = DEPTH // 2
LN_EPS = 1e-5
ALPHA = (2 * DEPTH) ** 0.25
BETA = (8 * DEPTH) ** -0.25
IN_COLS = 3 * W_A + 3 * W_B + H_B + 2 * D_MODEL

kernel_name = "moba_fox_hybrid_deepnorm_step"


def layer_norm(x, g, b):
    xf = x.astype(jnp.float32)
    mu = xf.mean(-1, keepdims=True)
    var = jnp.square(xf - mu).mean(-1, keepdims=True)
    return ((xf - mu) * lax.rsqrt(var + LN_EPS) * g + b).astype(x.dtype)


def rope(x, pos):
    half = HEAD_DIM // 2
    inv = ROPE_THETA ** (-jnp.arange(half, dtype=jnp.float32) / half)
    ang = pos.astype(jnp.float32)[:, None] * inv[None, :]
    cos = jnp.cos(ang)[None, :, None, :]
    sin = jnp.sin(ang)[None, :, None, :]
    x1 = x[..., :half].astype(jnp.float32)
    x2 = x[..., half:].astype(jnp.float32)
    return jnp.concatenate([x1 * cos - x2 * sin, x2 * cos + x1 * sin], axis=-1).astype(x.dtype)


def project_in(x, w_in, b_f):
    n, t = x.shape[:2]
    proj = jnp.einsum('btd,dc->btc', x, w_in)
    sizes = [W_A, W_A, W_A, W_B, W_B, W_B, H_B, D_MODEL, D_MODEL]
    cuts = [sum(sizes[:i + 1]) for i in range(len(sizes) - 1)]
    q_a, k_a, v_a, q_b, k_b, v_b, f_b, g_a, g_b = jnp.split(proj, cuts, axis=-1)
    ha = lambda z: z.reshape(n, t, H_A, HEAD_DIM)
    hb = lambda z: z.reshape(n, t, H_B, HEAD_DIM)
    logf = jax.nn.log_sigmoid((f_b + b_f).astype(jnp.float32))
    return ha(q_a), ha(k_a), ha(v_a), hb(q_b), hb(k_b), hb(v_b), logf, g_a, g_b


def merge_out(y_a, y_b, g_a, g_b, w_pa, w_pb, w_o):
    n, t = y_a.shape[:2]
    ya = jnp.einsum('btc,cd->btd', y_a.reshape(n, t, W_A), w_pa)
    yb = jnp.einsum('btc,cd->btd', y_b.reshape(n, t, W_B), w_pb)
    m = jax.nn.sigmoid(g_a) * ya + jax.nn.sigmoid(g_b) * yb
    return jnp.einsum('btd,de->bte', m, w_o)


def moba_core(q, pos, means, fetch, own_k, own_v, own_pos):
    n, tq = q.shape[:2]
    scale = HEAD_DIM ** -0.5
    cur_blk = pos // MOBA_BLOCK
    gate = jnp.einsum('nthd,nbhd->nthb', q.astype(jnp.float32), means.astype(jnp.float32))
    past = jnp.arange(means.shape[1])[None, :] < cur_blk[:, None]
    gate = jnp.where(past[None, :, None, :], gate, -jnp.inf)
    _, sel = lax.top_k(gate, MOBA_TOPK)
    sel_ok = sel < cur_blk[None, :, None, None]
    k_sel, v_sel = fetch(sel)
    s_sel = jnp.einsum('nthd,nthkjd->nthkj', q, k_sel).astype(jnp.float32) * scale
    s_sel = jnp.where(sel_ok[..., None], s_sel, -jnp.inf).reshape(n, tq, H_A, MOBA_TOPK * MOBA_BLOCK)
    s_own = jnp.einsum('nthd,nshd->nths', q, own_k).astype(jnp.float32) * scale
    own_ok = (own_pos[None, :] <= pos[:, None]) & (own_pos[None, :] >= cur_blk[:, None] * MOBA_BLOCK)
    s_own = jnp.where(own_ok[None, :, None, :], s_own, -jnp.inf)
    p = jax.nn.softmax(jnp.concatenate([s_sel, s_own], axis=-1), axis=-1)
    n_sel = MOBA_TOPK * MOBA_BLOCK
    p_sel = p[..., :n_sel].reshape(n, tq, H_A, MOBA_TOPK, MOBA_BLOCK).astype(v_sel.dtype)
    p_own = p[..., n_sel:].astype(own_v.dtype)
    return (jnp.einsum('nthkj,nthkjd->nthd', p_sel, v_sel)
            + jnp.einsum('nths,nshd->nthd', p_own, own_v))


def moba_prompt(q, k, v):
    b_sz, s_len = q.shape[:2]
    n_blk = max(-(-s_len // MOBA_BLOCK), MOBA_TOPK)
    pad = n_blk * MOBA_BLOCK + MOBA_BLOCK - s_len
    kp = jnp.pad(k, ((0, 0), (0, pad), (0, 0), (0, 0)))
    vp = jnp.pad(v, ((0, 0), (0, pad), (0, 0), (0, 0)))
    kb = kp[:, :n_blk * MOBA_BLOCK].reshape(b_sz, n_blk, MOBA_BLOCK, H_A, HEAD_DIM)
    vb = vp[:, :n_blk * MOBA_BLOCK].reshape(b_sz, n_blk, MOBA_BLOCK, H_A, HEAD_DIM)
    means = kb.astype(jnp.float32).mean(axis=2)
    n_qb = s_len // Q_BLOCK
    h_idx = jnp.arange(H_A)[None, None, :, None]

    def step(bc):
        bi, ci = bc
        q0 = ci * Q_BLOCK
        qc = lax.dynamic_slice(q, (bi, q0, 0, 0), (1, Q_BLOCK, H_A, HEAD_DIM))
        pos = q0 + jnp.arange(Q_BLOCK)
        bs = (q0 // MOBA_BLOCK) * MOBA_BLOCK
        own_k = lax.dynamic_slice(kp, (bi, bs, 0, 0), (1, MOBA_BLOCK, H_A, HEAD_DIM))
        own_v = lax.dynamic_slice(vp, (bi, bs, 0, 0), (1, MOBA_BLOCK, H_A, HEAD_DIM))
        own_pos = bs + jnp.arange(MOBA_BLOCK)
        mb = lax.dynamic_index_in_dim(means, bi, 0, keepdims=True)
        kb_b = lax.dynamic_index_in_dim(kb, bi, 0, keepdims=False)
        vb_b = lax.dynamic_index_in_dim(vb, bi, 0, keepdims=False)

        def fetch(sel):
            return kb_b[sel, :, h_idx, :], vb_b[sel, :, h_idx, :]

        return moba_core(qc, pos, mb, fetch, own_k, own_v, own_pos)

    b_ids = jnp.repeat(jnp.arange(b_sz, dtype=jnp.int32), n_qb)
    c_ids = jnp.tile(jnp.arange(n_qb, dtype=jnp.int32), b_sz)
    out = lax.map(step, (b_ids, c_ids))
    return out.reshape(b_sz, s_len, H_A, HEAD_DIM)


def moba_sample(q, k_new, v_new, cache_k, cache_v, page_table):
    n, t = q.shape[:2]
    n_pages = PAST_LEN // PAGE_SIZE
    n_past_blk = PAST_LEN // MOBA_BLOCK
    n_blk = max(n_past_blk, MOBA_TOPK)
    blk_pages = page_table[:, :n_past_blk * PAGES_PER_BLOCK].reshape(n, n_past_blk, PAGES_PER_BLOCK)
    blk_pages = jnp.pad(blk_pages, ((0, 0), (0, n_blk - n_past_blk), (0, 0)))
    means = lax.map(lambda bp: cache_k[bp].astype(jnp.float32).mean(axis=(1, 2)), blk_pages)
    bs = n_past_blk * MOBA_BLOCK
    r = PAST_LEN - bs
    own_pages = page_table[:, bs // PAGE_SIZE:n_pages]
    own_k = jnp.concatenate([cache_k[own_pages].reshape(n, r, H_A, HEAD_DIM), k_new], axis=1)
    own_v = jnp.concatenate([cache_v[own_pages].reshape(n, r, H_A, HEAD_DIM), v_new], axis=1)
    own_pos = bs + jnp.arange(r + t)
    pos = PAST_LEN + jnp.arange(t)
    n_idx = jnp.arange(n)[:, None, None, None]
    h_idx = jnp.arange(H_A)[None, None, :, None, None]

    def fetch(sel):
        phys = blk_pages[n_idx, sel]
        ks = cache_k[phys, :, h_idx, :].reshape(n, t, H_A, MOBA_TOPK, MOBA_BLOCK, HEAD_DIM)
        vs = cache_v[phys, :, h_idx, :].reshape(n, t, H_A, MOBA_TOPK, MOBA_BLOCK, HEAD_DIM)
        return ks, vs

    return moba_core(q, pos, means, fetch, own_k, own_v, own_pos)


def fox_core(q, k, v, cq, ck, pos_q, pos_k):
    scale = HEAD_DIM ** -0.5
    s = jnp.einsum('nthd,nshd->nhts', q, k).astype(jnp.float32) * scale
    s = s + (jnp.transpose(cq, (0, 2, 1))[..., :, None] - jnp.transpose(ck, (0, 2, 1))[..., None, :])
    s = jnp.where(pos_k[None, :] <= pos_q[:, None], s, -jnp.inf)
    p = jax.nn.softmax(s, axis=-1).astype(v.dtype)
    return jnp.einsum('nhts,nshd->nthd', p, v)


def fox_prompt(q, k, v, logf):
    b_sz, s_len = q.shape[:2]
    c = jnp.cumsum(logf, axis=1)
    pos_k = jnp.arange(s_len)
    n_qb = s_len // Q_BLOCK
    qs = jnp.transpose(q.reshape(b_sz, n_qb, Q_BLOCK, H_B, HEAD_DIM), (1, 0, 2, 3, 4))
    cs = jnp.transpose(c.reshape(b_sz, n_qb, Q_BLOCK, H_B), (1, 0, 2, 3))

    def step(args):
        i, qc, cq = args
        pos_q = i * Q_BLOCK + jnp.arange(Q_BLOCK)
        return fox_core(qc, k, v, cq, c, pos_q, pos_k)

    out = lax.map(step, (jnp.arange(n_qb), qs, cs))
    return jnp.transpose(out, (1, 0, 2, 3, 4)).reshape(b_sz, s_len, H_B, HEAD_DIM)


def fox_sample(q, k_new, v_new, logf_new, cache_k, cache_v, cache_logf, page_table):
    n, t = q.shape[:2]
    logf_past = cache_logf[page_table].reshape(n, PAST_LEN, H_B).astype(jnp.float32)
    c_all = jnp.cumsum(jnp.concatenate([logf_past, logf_new], axis=1), axis=1)
    pos_q = PAST_LEN + jnp.arange(t)
    pos_k = jnp.arange(PAST_LEN + t)

    def one(args):
        pt, qn, kn, vn, cq, ck = args
        k_all = jnp.concatenate([cache_k[pt].reshape(PAST_LEN, H_B, HEAD_DIM), kn], axis=0)[None]
        v_all = jnp.concatenate([cache_v[pt].reshape(PAST_LEN, H_B, HEAD_DIM), vn], axis=0)[None]
        return fox_core(qn[None], k_all, v_all, cq[None], ck[None], pos_q, pos_k)[0]

    return lax.map(one, (page_table, q, k_new, v_new, c_all[:, PAST_LEN:], c_all))


def mixer_prompt(x, w_in, b_f, w_pa, w_pb, w_o):
    pos = jnp.arange(x.shape[1])
    q_a, k_a, v_a, q_b, k_b, v_b, logf, g_a, g_b = project_in(x, w_in, b_f)
    q_a = rope(q_a, pos)
    k_a = rope(k_a, pos)
    y_a = moba_prompt(q_a, k_a, v_a)
    y_b = fox_prompt(q_b, k_b, v_b, logf)
    return merge_out(y_a, y_b, g_a, g_b, w_pa, w_pb, w_o), [k_a, v_a, k_b, v_b, logf]


def mixer_sample(x, cache_k_a, cache_v_a, cache_k_b, cache_v_b, cache_logf_b, page_table, w_in, b_f, w_pa, w_pb, w_o):
    pos = PAST_LEN + jnp.arange(x.shape[1])
    q_a, k_a, v_a, q_b, k_b, v_b, logf, g_a, g_b = project_in(x, w_in, b_f)
    q_a = rope(q_a, pos)
    k_a = rope(k_a, pos)
    y_a = moba_sample(q_a, k_a, v_a, cache_k_a, cache_v_a, page_table)
    y_b = fox_sample(q_b, k_b, v_b, logf, cache_k_b, cache_v_b, cache_logf_b, page_table)
    return merge_out(y_a, y_b, g_a, g_b, w_pa, w_pb, w_o), [k_a, v_a, k_b, v_b, logf]


def swiglu(x, wg, wu, wd):
    h = jax.nn.silu(jnp.einsum('btd,df->btf', x, wg)) * jnp.einsum('btd,df->btf', x, wu)
    return jnp.einsum('btf,fd->btd', h, wd)


def moe_swiglu(x, w_router, wg, wu, wd):
    logits = jnp.einsum('btd,de->bte', x, w_router).astype(jnp.float32)
    top_val, top_idx = lax.top_k(logits, TOP_K)
    top_w = jax.nn.softmax(top_val, axis=-1)
    gates = jnp.einsum('btk,btke->bte', top_w, jax.nn.one_hot(top_idx, N_EXPERTS, dtype=jnp.float32)).astype(x.dtype)
    y = jnp.zeros_like(x)
    for e in range(N_EXPERTS):
        y = y + gates[..., e:e + 1] * swiglu(x, wg[e], wu[e], wd[e])
    return y


def channel_mixer(x, l, w_gate_d, w_up_d, w_down_d, w_router, w_gate_e, w_up_e, w_down_e):
    i = l // 2
    if l % 2 == 0:
        return swiglu(x, w_gate_d[i], w_up_d[i], w_down_d[i])
    return moe_swiglu(x, w_router[i], w_gate_e[i], w_up_e[i], w_down_e[i])


def setup_inputs(seed: int = 0) -> dict:
    key = jax.random.key(seed)
    ks = jax.random.split(key, 24)
    nrm = jax.random.normal
    n_pages = PAST_LEN // PAGE_SIZE
    n_pool = (DEC_BATCH * n_pages * 5) // 4
    x_prompt = nrm(ks[0], (BATCH, SEQ, D_MODEL), jnp.float32)
    x_sample = nrm(ks[1], (DEC_BATCH, DEC_SEQ, D_MODEL), jnp.float32)
    cache_k_a = nrm(ks[2], (DEPTH, n_pool, PAGE_SIZE, H_A, HEAD_DIM), jnp.float32)
    cache_v_a = nrm(ks[3], (DEPTH, n_pool, PAGE_SIZE, H_A, HEAD_DIM), jnp.float32)
    cache_k_b = nrm(ks[4], (DEPTH, n_pool, PAGE_SIZE, H_B, HEAD_DIM), jnp.float32)
    cache_v_b = nrm(ks[5], (DEPTH, n_pool, PAGE_SIZE, H_B, HEAD_DIM), jnp.float32)
    cache_logf_b = jax.nn.log_sigmoid(4.0 + nrm(ks[6], (DEPTH, n_pool, PAGE_SIZE, H_B), jnp.float32))
    perm = jax.random.permutation(ks[7], n_pool)
    page_table = perm[:DEC_BATCH * n_pages].reshape(DEC_BATCH, n_pages).astype(jnp.int32)
    w_in = nrm(ks[8], (DEPTH, D_MODEL, IN_COLS), jnp.float32) * D_MODEL ** -0.5
    b_f = jnp.linspace(2.0, 6.0, H_B, dtype=jnp.float32)[None, :] + 0.1 * nrm(ks[9], (DEPTH, H_B), jnp.float32)
    w_pa = nrm(ks[10], (DEPTH, W_A, D_MODEL), jnp.float32) * W_A ** -0.5
    w_pb = nrm(ks[11], (DEPTH, W_B, D_MODEL), jnp.float32) * W_B ** -0.5
    w_o = nrm(ks[12], (DEPTH, D_MODEL, D_MODEL), jnp.float32) * (D_MODEL ** -0.5 * BETA)
    ln_g = 1.0 + 0.05 * nrm(ks[13], (DEPTH, 2, D_MODEL), jnp.float32)
    ln_b = 0.02 * nrm(ks[14], (DEPTH, 2, D_MODEL), jnp.float32)
    w_gate_d = nrm(ks[15], (N_DENSE, D_MODEL, D_FF), jnp.float32) * D_MODEL ** -0.5
    w_up_d = nrm(ks[16], (N_DENSE, D_MODEL, D_FF), jnp.float32) * D_MODEL ** -0.5
    w_down_d = nrm(ks[17], (N_DENSE, D_FF, D_MODEL), jnp.float32) * (D_FF ** -0.5 * BETA)
    w_router = nrm(ks[18], (N_MOE, D_MODEL, N_EXPERTS), jnp.float32) * D_MODEL ** -0.5
    w_gate_e = nrm(ks[19], (N_MOE, N_EXPERTS, D_MODEL, D_FF), jnp.float32) * D_MODEL ** -0.5
    w_up_e = nrm(ks[20], (N_MOE, N_EXPERTS, D_MODEL, D_FF), jnp.float32) * D_MODEL ** -0.5
    w_down_e = nrm(ks[21], (N_MOE, N_EXPERTS, D_FF, D_MODEL), jnp.float32) * (D_FF ** -0.5 * BETA)
    return {"x_prompt": x_prompt, "x_sample": x_sample,
            "cache_k_a": cache_k_a, "cache_v_a": cache_v_a,
            "cache_k_b": cache_k_b, "cache_v_b": cache_v_b, "cache_logf_b": cache_logf_b,
            "page_table": page_table,
            "w_in": w_in, "b_f": b_f, "w_pa": w_pa, "w_pb": w_pb, "w_o": w_o,
            "ln_g": ln_g, "ln_b": ln_b,
            "w_gate_d": w_gate_d, "w_up_d": w_up_d, "w_down_d": w_down_d,
            "w_router": w_router, "w_gate_e": w_gate_e, "w_up_e": w_up_e, "w_down_e": w_down_e}


def reference(x_prompt, x_sample, cache_k_a, cache_v_a, cache_k_b, cache_v_b, cache_logf_b, page_table,
              w_in, b_f, w_pa, w_pb, w_o, ln_g, ln_b,
              w_gate_d, w_up_d, w_down_d, w_router, w_gate_e, w_up_e, w_down_e):
    xp = x_prompt
    xs = x_sample
    rows_p = [[], [], [], [], []]
    rows_s = [[], [], [], [], []]
    for l in range(DEPTH):
        hp, st_p = mixer_prompt(xp, w_in[l], b_f[l], w_pa[l], w_pb[l], w_o[l])
        hs, st_s = mixer_sample(xs, cache_k_a[l], cache_v_a[l], cache_k_b[l], cache_v_b[l], cache_logf_b[l],
                                page_table, w_in[l], b_f[l], w_pa[l], w_pb[l], w_o[l])
        xp = layer_norm(ALPHA * xp + hp, ln_g[l, 0], ln_b[l, 0])
        xs = layer_norm(ALPHA * xs + hs, ln_g[l, 0], ln_b[l, 0])
        fp = channel_mixer(xp, l, w_gate_d, w_up_d, w_down_d, w_router, w_gate_e, w_up_e, w_down_e)
        fs = channel_mixer(xs, l, w_gate_d, w_up_d, w_down_d, w_router, w_gate_e, w_up_e, w_down_e)
        xp = layer_norm(ALPHA * xp + fp, ln_g[l, 1], ln_b[l, 1])
        xs = layer_norm(ALPHA * xs + fs, ln_g[l, 1], ln_b[l, 1])
        for j in range(5):
            rows_p[j].append(st_p[j])
            rows_s[j].append(st_s[j])
    k_a_prompt = jnp.stack(rows_p[0])
    v_a_prompt = jnp.stack(rows_p[1])
    k_b_prompt = jnp.stack(rows_p[2])
    v_b_prompt = jnp.stack(rows_p[3])
    logf_b_prompt = jnp.stack(rows_p[4])
    k_a_sample = jnp.stack(rows_s[0])
    v_a_sample = jnp.stack(rows_s[1])
    k_b_sample = jnp.stack(rows_s[2])
    v_b_sample = jnp.stack(rows_s[3])
    logf_b_sample = jnp.stack(rows_s[4])
    return (xp, xs, k_a_prompt, v_a_prompt, k_b_prompt, v_b_prompt, logf_b_prompt,
            k_a_sample, v_a_sample, k_b_sample, v_b_sample, logf_b_sample)
```

```python
import functools

import jax
import jax.numpy as jnp
from jax import lax
from jax.experimental import pallas as pl
from jax.experimental.pallas import tpu as pltpu

HEAD_DIM = 64
MOBA_BLOCK = 256
MOBA_TOPK = 3
TOP_K = 2
ROPE_THETA = 10000.0
LN_EPS = 1e-5
NEG = -1e30
LANES = 128
GROUP = 512
N_GROUPS = 10
VMEM_LIMIT = 56 * 1024 * 1024
HI = lax.Precision.HIGHEST
F32 = jnp.float32
BF16 = jnp.bfloat16


def _dot(a, b, hp):
    if hp:
        return jnp.dot(a, b, precision=HI, preferred_element_type=F32)
    return jnp.dot(a, b, preferred_element_type=F32)


def _dot_nt(a, b, hp):
    dn = (((1,), (1,)), ((), ()))
    if hp:
        return lax.dot_general(a, b, dn, precision=HI, preferred_element_type=F32)
    return lax.dot_general(a, b, dn, preferred_element_type=F32)


def _params(sem):
    return pltpu.CompilerParams(dimension_semantics=sem, vmem_limit_bytes=VMEM_LIMIT)


def _layer_norm(z, g, b):
    mu = jnp.mean(z, axis=-1, keepdims=True)
    zc = z - mu
    var = jnp.mean(zc * zc, axis=-1, keepdims=True)
    return zc * lax.rsqrt(var + LN_EPS) * g + b


def _rope_cols(r, cos, sin):
    lane = lax.broadcasted_iota(jnp.int32, cos.shape, 1)
    first = (lane % HEAD_DIM) < (HEAD_DIM // 2)
    out = []
    for c in range(GROUP // LANES):
        seg = r[:, c * LANES:(c + 1) * LANES]
        rot = jnp.where(first, pltpu.roll(seg, LANES - HEAD_DIM // 2, 1), pltpu.roll(seg, HEAD_DIM // 2, 1))
        out.append(seg * cos + rot * sin)
    return jnp.concatenate(out, axis=1)


def _proj_kernel(x_ref, w_ref, wf_ref, bf_ref, cos_ref, sin_ref, p_ref, lf_ref, xb_ref, *, hp):
    j = pl.program_id(1)
    scale = HEAD_DIM ** -0.5

    @pl.when(j == 0)
    def _():
        xb_ref[...] = x_ref[...].astype(xb_ref.dtype)
        z = _dot(xb_ref[...], wf_ref[...], hp) + bf_ref[...]
        lf_ref[...] = jnp.minimum(z, 0.0) - jnp.log1p(jnp.exp(-jnp.abs(z)))

    r = _dot(xb_ref[...], w_ref[...], hp)

    @pl.when(j == 0)
    def _():
        p_ref[...] = _rope_cols(r, cos_ref[...], sin_ref[...]) * scale

    @pl.when(j == 1)
    def _():
        p_ref[...] = _rope_cols(r, cos_ref[...], sin_ref[...])

    @pl.when(j == 3)
    def _():
        p_ref[...] = r * scale

    @pl.when((j == 2) | (j == 4) | (j == 5))
    def _():
        p_ref[...] = r

    @pl.when(j >= 6)
    def _():
        p_ref[...] = jax.nn.sigmoid(r)


def _project(x, w, wf, bfp, cos, sin, *, tm, table_tiles, hp):
    n, d = x.shape
    cdt = F32 if hp else BF16
    return pl.pallas_call(
        functools.partial(_proj_kernel, hp=hp),
        grid=(n // tm, N_GROUPS),
        in_specs=[
            pl.BlockSpec((tm, d), lambda i, j: (i, 0)),
            pl.BlockSpec((d, GROUP), lambda i, j: (0, j)),
            pl.BlockSpec((d, LANES), lambda i, j: (0, 0)),
            pl.BlockSpec((1, LANES), lambda i, j: (0, 0)),
            pl.BlockSpec((tm, LANES), lambda i, j: (i % table_tiles, 0)),
            pl.BlockSpec((tm, LANES), lambda i, j: (i % table_tiles, 0)),
        ],
        out_specs=[
            pl.BlockSpec((tm, GROUP), lambda i, j: (i, j)),
            pl.BlockSpec((tm, LANES), lambda i, j: (i, 0)),
        ],
        out_shape=[jax.ShapeDtypeStruct((n, N_GROUPS * GROUP), F32),
                   jax.ShapeDtypeStruct((n, LANES), F32)],
        scratch_shapes=[pltpu.VMEM((tm, d), cdt)],
        compiler_params=_params(("parallel", "arbitrary")),
        name="project_in",
    )(x, w, wf, bfp, cos, sin)


def _cumsum_kernel(lf_ref, c_ref):
    ch = 256
    s_len = lf_ref.shape[0]
    row = lax.broadcasted_iota(jnp.int32, (ch, ch), 0)
    col = lax.broadcasted_iota(jnp.int32, (ch, ch), 1)
    tri = (col <= row).astype(F32)

    def body(k, carry):
        rows = pl.ds(pl.multiple_of(k * ch, ch), ch)
        c = _dot(tri, lf_ref[rows, :], True) + carry
        c_ref[rows, :] = c
        return c[ch - 1:ch, :]

    lax.fori_loop(0, s_len // ch, body, jnp.zeros((1, LANES), F32))


def _cumsum(lf3):
    b, s, _ = lf3.shape
    return pl.pallas_call(
        _cumsum_kernel,
        grid=(b,),
        in_specs=[pl.BlockSpec((None, s, LANES), lambda i: (i, 0, 0))],
        out_specs=pl.BlockSpec((None, s, LANES), lambda i: (i, 0, 0)),
        out_shape=jax.ShapeDtypeStruct(lf3.shape, F32),
        compiler_params=_params(("parallel",)),
        name="fox_cumsum",
    )(lf3)


def _split3(c):
    hi = c.astype(BF16).astype(F32)
    mid = (c - hi).astype(BF16).astype(F32)
    lo = c - hi - mid
    return hi, mid, lo


def _place(lane, base, vals):
    out = jnp.zeros(lane.shape, F32)
    for k, v in enumerate(vals):
        out = jnp.where(lane == base + k, v, out)
    return out


def _attn_prompt_kernel(*refs, mode, tq, tk, n_blk):
    if mode == "fox":
        q_ref, k_ref, v_ref, c_ref, y_ref, kx_ref, vb_ref, qx_ref, m_ref, l_ref, acc_ref = refs
    else:
        q_ref, k_ref, v_ref, y_ref, kx_ref, vb_ref, qx_ref, m_ref, l_ref, acc_ref, mx_ref = refs
    hp = pl.program_id(1)
    i = pl.program_id(2)
    s_len = k_ref.shape[0]
    pc = 512
    bases = (HEAD_DIM, 0)

    @pl.when(i == 0)
    def _():
        if mode == "moba":
            mx_ref[...] = jnp.zeros(mx_ref.shape, F32)

        def body(t, _):
            rows = pl.ds(pl.multiple_of(t * pc, pc), pc)
            kc = k_ref[rows, :]
            lane = lax.broadcasted_iota(jnp.int32, (pc, LANES), 1)
            vb_ref[rows, :] = v_ref[rows, :].astype(BF16)
            for head in range(2):
                base = bases[head]
                if mode == "fox":
                    h = 2 * hp + head
                    ck = jnp.sum(jnp.where(lane == h, c_ref[rows, :], 0.0), axis=1, keepdims=True)
                    hi, mid, lo = _split3(ck)
                    ex = _place(lane, base, [1.0, 1.0, 1.0, -hi, -mid, -lo])
                else:
                    blk = (t * pc + lax.broadcasted_iota(jnp.int32, (pc, LANES), 0)) // MOBA_BLOCK
                    ex = jnp.where(lane - base == blk, 1.0, 0.0)
                    for u in range(pc // MOBA_BLOCK):
                        mean = jnp.sum(kc[u * MOBA_BLOCK:(u + 1) * MOBA_BLOCK, :], axis=0, keepdims=True)
                        mx_ref[head, pl.ds(base + t * (pc // MOBA_BLOCK) + u, 1), :] = mean * (1.0 / MOBA_BLOCK)
                kx_ref[head, rows, :] = jnp.where(lane // HEAD_DIM == head, kc, ex).astype(BF16)
            return 0

        lax.fori_loop(0, s_len // pc, body, 0)

    qq = q_ref[...]
    lane = lax.broadcasted_iota(jnp.int32, (tq, LANES), 1)
    qpos = i * tq + lax.broadcasted_iota(jnp.int32, (tq, LANES), 0)
    for head in range(2):
        base = bases[head]
        if mode == "fox":
            h = 2 * hp + head
            cq = jnp.sum(jnp.where(lane == h, c_ref[pl.ds(pl.multiple_of(i * tq, tq), tq), :], 0.0),
                         axis=1, keepdims=True)
            hi, mid, lo = _split3(cq)
            ex = _place(lane, base, [hi, mid, lo, 1.0, 1.0, 1.0])
        else:
            qh = jnp.where(lane // HEAD_DIM == head, qq, 0.0)
            gate = _dot_nt(qh, mx_ref[head], True)
            nidx = lane - base
            cb = qpos // MOBA_BLOCK
            valid = (nidx >= 0) & (nidx < cb)
            rank = jnp.zeros((tq, LANES), F32)
            for n2 in range(n_blk):
                g2 = gate[:, base + n2:base + n2 + 1]
                beats = (g2 > gate) | ((g2 == gate) & (n2 < nidx))
                rank = rank + jnp.where(beats & (n2 < cb), 1.0, 0.0)
            keep = (valid & (rank < MOBA_TOPK)) | (nidx == cb)
            ex = jnp.where(keep, 0.0, NEG)
        qx_ref[head] = jnp.where(lane // HEAD_DIM == head, qq, ex).astype(BF16)
    m_ref[...] = jnp.full(m_ref.shape, NEG, F32)
    l_ref[...] = jnp.zeros(l_ref.shape, F32)
    acc_ref[...] = jnp.zeros(acc_ref.shape, F32)

    def chunk(j, head, masked):
        rows = pl.ds(pl.multiple_of(j * tk, tk), tk)
        s = _dot_nt(qx_ref[head], kx_ref[head, rows, :], False)
        if masked:
            kpos = j * tk + lax.broadcasted_iota(jnp.int32, (tq, tk), 1)
            qp = i * tq + lax.broadcasted_iota(jnp.int32, (tq, tk), 0)
            s = jnp.where(kpos <= qp, s, NEG)
        m_prev = m_ref[head]
        m_new = jnp.maximum(m_prev, jnp.max(s, axis=1, keepdims=True))
        p = jnp.exp(s - jnp.tile(m_new, (1, tk // LANES)))
        alpha = jnp.exp(m_prev - m_new)
        l_ref[head] = alpha * l_ref[head] + jnp.sum(p, axis=1, keepdims=True)
        acc_ref[head] = alpha * acc_ref[head] + _dot(p.astype(BF16), vb_ref[rows, :], False)
        m_ref[head] = m_new

    n_full = i * (tq // tk)
    for head in range(2):
        def body(j, _, head=head):
            chunk(j, head, False)
            return 0
        lax.fori_loop(0, n_full, body, 0)
        for d in range(tq // tk):
            chunk(n_full + d, head, True)

    o0 = acc_ref[0] / l_ref[0]
    o1 = acc_ref[1] / l_ref[1]
    y_ref[...] = jnp.where(lane // HEAD_DIM == 0, o0, o1).astype(y_ref.dtype)


def _attn_prompt(p3, c3, *, mode, qg, kg, vg, tq=512, tk=256):
    b, s, _ = p3.shape
    pairs = GROUP // LANES
    n_blk = s // MOBA_BLOCK
    assert s % tq == 0 and tq % tk == 0 and tk == MOBA_BLOCK and n_blk <= HEAD_DIM
    in_specs = [
        pl.BlockSpec((None, tq, LANES), lambda bi, h, i: (bi, i, qg * pairs + h)),
        pl.BlockSpec((None, s, LANES), lambda bi, h, i: (bi, 0, kg * pairs + h)),
        pl.BlockSpec((None, s, LANES), lambda bi, h, i: (bi, 0, vg * pairs + h)),
    ]
    args = [p3, p3, p3]
    scratch = [pltpu.VMEM((2, s, LANES), BF16), pltpu.VMEM((s, LANES), BF16), pltpu.VMEM((2, tq, LANES), BF16),
               pltpu.VMEM((2, tq, LANES), F32), pltpu.VMEM((2, tq, LANES), F32), pltpu.VMEM((2, tq, LANES), F32)]
    if mode == "fox":
        in_specs.append(pl.BlockSpec((None, s, LANES), lambda bi, h, i: (bi, 0, 0)))
        args.append(c3)
    else:
        scratch.append(pltpu.VMEM((2, LANES, LANES), F32))
    return pl.pallas_call(
        functools.partial(_attn_prompt_kernel, mode=mode, tq=tq, tk=tk, n_blk=n_blk),
        grid=(b, pairs, s // tq),
        in_specs=in_specs,
        out_specs=pl.BlockSpec((None, tq, LANES), lambda bi, h, i: (bi, i, h)),
        out_shape=jax.ShapeDtypeStruct((b, s, GROUP), BF16),
        scratch_shapes=scratch,
        compiler_params=_params(("parallel", "parallel", "arbitrary")),
        name=mode + "_prompt",
    )(*args)


def _merge_kernel(*refs, hp, alpha, n_exp):
    if n_exp:
        (ya_ref, yb_ref, ga_ref, gb_ref, x_ref, wpa_ref, wpb_ref, wo_ref, g_ref, b_ref, wr_ref,
         o_ref, idx_ref, wt_ref) = refs
    else:
        ya_ref, yb_ref, ga_ref, gb_ref, x_ref, wpa_ref, wpb_ref, wo_ref, g_ref, b_ref, o_ref = refs
    cdt = F32 if hp else BF16
    m = (ga_ref[...] * _dot(ya_ref[...], wpa_ref[...], hp)
         + gb_ref[...] * _dot(yb_ref[...], wpb_ref[...], hp))
    h = _dot(m.astype(cdt), wo_ref[...], hp)
    x1 = _layer_norm(alpha * x_ref[...] + h, g_ref[...], b_ref[...])
    o_ref[...] = x1
    if n_exp:
        logits = _dot(x1, wr_ref[...], True)
        lane = lax.broadcasted_iota(jnp.int32, logits.shape, 1)
        logits = jnp.where(lane < n_exp, logits, NEG)
        v1 = jnp.max(logits, axis=1, keepdims=True)
        i1 = jnp.min(jnp.where(logits == v1, lane, LANES), axis=1, keepdims=True)
        rest = jnp.where(lane == i1, NEG, logits)
        v2 = jnp.max(rest, axis=1, keepdims=True)
        i2 = jnp.min(jnp.where(rest == v2, lane, LANES), axis=1, keepdims=True)
        e = jnp.exp(v2 - v1)
        w1 = 1.0 / (1.0 + e)
        w2 = e / (1.0 + e)
        idx_ref[...] = jnp.where(lane == 0, i1, jnp.where(lane == 1, i2, 0))
        wt_ref[...] = jnp.where(lane == 0, w1, jnp.where(lane == 1, w2, 0.0))


def _merge(ya, yb, p, x, wpa, wpb, wo, g, b, wr, *, tm, hp, alpha, n_exp):
    n, d = x.shape
    assert d == 2 * GROUP
    ga_blk, gb_blk = 3, 4
    in_specs = [
        pl.BlockSpec((tm, GROUP), lambda i: (i, 0)),
        pl.BlockSpec((tm, GROUP), lambda i: (i, 0)),
        pl.BlockSpec((tm, d), lambda i: (i, ga_blk)),
        pl.BlockSpec((tm, d), lambda i: (i, gb_blk)),
        pl.BlockSpec((tm, d), lambda i: (i, 0)),
        pl.BlockSpec((GROUP, d), lambda i: (0, 0)),
        pl.BlockSpec((GROUP, d), lambda i: (0, 0)),
        pl.BlockSpec((d, d), lambda i: (0, 0)),
        pl.BlockSpec((1, d), lambda i: (0, 0)),
        pl.BlockSpec((1, d), lambda i: (0, 0)),
    ]
    args = [ya, yb, p, p, x, wpa, wpb, wo, g, b]
    out_specs = [pl.BlockSpec((tm, d), lambda i: (i, 0))]
    out_shape = [jax.ShapeDtypeStruct((n, d), F32)]
    if n_exp:
        in_specs.append(pl.BlockSpec((d, LANES), lambda i: (0, 0)))
        args.append(wr)
        out_specs += [pl.BlockSpec((tm, LANES), lambda i: (i, 0))] * 2
        out_shape += [jax.ShapeDtypeStruct((n, LANES), jnp.int32), jax.ShapeDtypeStruct((n, LANES), F32)]
    return pl.pallas_call(
        functools.partial(_merge_kernel, hp=hp, alpha=alpha, n_exp=n_exp),
        grid=(n // tm,),
        in_specs=in_specs,
        out_specs=out_specs,
        out_shape=out_shape,
        compiler_params=_params(("parallel",)),
        name="merge_out",
    )(*args)


def _ffn_kernel(te_ref, nv_ref, x_ref, sc_ref, wg_ref, wu_ref, wd_ref, g_ref, b_ref, o_ref, xb_ref, acc_ref,
                *, hp, alpha, fuse_ln):
    i = pl.program_id(0)
    f = pl.program_id(1)
    cdt = F32 if hp else BF16
    live = i < nv_ref[0]

    @pl.when(f == 0)
    def _():
        xb_ref[...] = x_ref[...].astype(cdt)
        acc_ref[...] = jnp.zeros(acc_ref.shape, F32)

    @pl.when(live)
    def _():
        xb = xb_ref[...]
        a = _dot(xb, wg_ref[...], hp)
        u = _dot(xb, wu_ref[...], hp)
        hmid = a * jax.nn.sigmoid(a) * u
        acc_ref[...] += _dot(hmid.astype(cdt), wd_ref[...], hp)

    @pl.when(f == pl.num_programs(1) - 1)
    def _():
        if fuse_ln:
            o_ref[...] = _layer_norm(alpha * x_ref[...] + acc_ref[...], g_ref[...], b_ref[...])
        else:
            o_ref[...] = acc_ref[...] * sc_ref[...]


def _ffn(x, scale, tile_expert, n_live, wg, wu, wd, g, b, *, tm, tf, hp, alpha, fuse_ln):
    r, d = x.shape
    _, _, dff = wg.shape
    grid_spec = pltpu.PrefetchScalarGridSpec(
        num_scalar_prefetch=2,
        grid=(r // tm, dff // tf),
        in_specs=[
            pl.BlockSpec((tm, d), lambda i, f, te, nv: (i, 0)),
            pl.BlockSpec((tm, 1), lambda i, f, te, nv: (i, 0)),
            pl.BlockSpec((None, d, tf), lambda i, f, te, nv: (te[i], 0, f)),
            pl.BlockSpec((None, d, tf), lambda i, f, te, nv: (te[i], 0, f)),
            pl.BlockSpec((None, tf, d), lambda i, f, te, nv: (te[i], f, 0)),
            pl.BlockSpec((1, d), lambda i, f, te, nv: (0, 0)),
            pl.BlockSpec((1, d), lambda i, f, te, nv: (0, 0)),
        ],
        out_specs=pl.BlockSpec((tm, d), lambda i, f, te, nv: (i, 0)),
        scratch_shapes=[pltpu.VMEM((tm, d), F32 if hp else BF16), pltpu.VMEM((tm, d), F32)],
    )
    return pl.pallas_call(
        functools.partial(_ffn_kernel, hp=hp, alpha=alpha, fuse_ln=fuse_ln),
        grid_spec=grid_spec,
        out_shape=jax.ShapeDtypeStruct((r, d), F32),
        compiler_params=_params(("parallel", "arbitrary")),
        name="swiglu",
    )(tile_expert, n_live, x, scale, wg, wu, wd, g, b)


def _combine_kernel(x_ref, y1_ref, y2_ref, g_ref, b_ref, o_ref, *, alpha):
    o_ref[...] = _layer_norm(alpha * x_ref[...] + (y1_ref[...] + y2_ref[...]), g_ref[...], b_ref[...])


def _combine(x, y1, y2, g, b, *, tm, alpha):
    n, d = x.shape
    row = pl.BlockSpec((tm, d), lambda i: (i, 0))
    vec = pl.BlockSpec((1, d), lambda i: (0, 0))
    return pl.pallas_call(
        functools.partial(_combine_kernel, alpha=alpha),
        grid=(n // tm,),
        in_specs=[row, row, row, vec, vec],
        out_specs=row,
        out_shape=jax.ShapeDtypeStruct((n, d), F32),
        compiler_params=_params(("parallel",)),
        name="moe_combine",
    )(x, y1, y2, g, b)


PAGE_ROWS = 128


def _page_specs(layer, g_pages):
    def spec(g):
        return pl.BlockSpec((None, None, PAGE_ROWS, GROUP),
                            lambda n, c, pt: (layer, pt[n, c * g_pages + g], 0, 0))
    return [spec(g) for g in range(g_pages)]


def _means_kernel(pt_ref, *refs, g_pages):
    k_refs = refs[:g_pages]
    o_ref = refs[g_pages]
    ppb = MOBA_BLOCK // PAGE_ROWS
    for blk in range(g_pages // ppb):
        tot = jnp.zeros((1, GROUP), F32)
        for u in range(ppb):
            tot = tot + jnp.sum(k_refs[blk * ppb + u][...], axis=0, keepdims=True)
        o_ref[blk:blk + 1, :] = tot * (1.0 / MOBA_BLOCK)


def _moba_means(cache_k, page_table, layer, *, g_pages=16):
    n, n_pages = page_table.shape
    ppb = MOBA_BLOCK // PAGE_ROWS
    n_blk = n_pages // ppb
    grid_spec = pltpu.PrefetchScalarGridSpec(
        num_scalar_prefetch=1,
        grid=(n, n_pages // g_pages),
        in_specs=_page_specs(layer, g_pages),
        out_specs=pl.BlockSpec((None, g_pages // ppb, GROUP), lambda i, c, pt: (i, c, 0)),
    )
    return pl.pallas_call(
        functools.partial(_means_kernel, g_pages=g_pages),
        grid_spec=grid_spec,
        out_shape=jax.ShapeDtypeStruct((n, n_blk, GROUP), F32),
        compiler_params=_params(("parallel", "arbitrary")),
        name="moba_sample_means",
    )(page_table, *([cache_k] * g_pages))


def _build_qbd(q, t_new):
    n_heads = GROUP // HEAD_DIM
    lane_head = lax.broadcasted_iota(jnp.int32, (n_heads, GROUP), 1) // HEAD_DIM
    row_head = lax.broadcasted_iota(jnp.int32, (n_heads, GROUP), 0)
    parts = [jnp.where(lane_head == row_head, jnp.broadcast_to(q[qi:qi + 1, :], (n_heads, GROUP)), 0.0)
             for qi in range(t_new)]
    return jnp.concatenate(parts, axis=0)


def _moba_gate_kernel(q_ref, mean_ref, o_ref, *, t_new, n_blk, past):
    n_heads = GROUP // HEAD_DIM
    rows = t_new * n_heads
    qbd = _build_qbd(q_ref[...], t_new)
    gate = _dot_nt(qbd, mean_ref[...], True)
    lane = lax.broadcasted_iota(jnp.int32, (rows, n_blk), 1)
    rank = jnp.zeros((rows, n_blk), F32)
    for n2 in range(n_blk):
        g2 = gate[:, n2:n2 + 1]
        beats = (g2 > gate) | ((g2 == gate) & (n2 < lane))
        rank = rank + jnp.where(beats, 1.0, 0.0)
    sel = jnp.where(rank < MOBA_TOPK, 1.0, 0.0)
    ck = 2048
    for c in range(past // ck):
        kblk = (c * ck + lax.broadcasted_iota(jnp.int32, (n_blk, ck), 1)) // MOBA_BLOCK
        expand = jnp.where(kblk == lax.broadcasted_iota(jnp.int32, (n_blk, ck), 0), 1.0, 0.0)
        hit = _dot(sel, expand, True)
        o_ref[:, c * ck:(c + 1) * ck] = jnp.where(hit > 0.5, 0.0, NEG)


def _moba_gate(p3, means, *, past):
    n, t_new, _ = p3.shape
    n_blk = means.shape[1]
    rows = t_new * (GROUP // HEAD_DIM)
    return pl.pallas_call(
        functools.partial(_moba_gate_kernel, t_new=t_new, n_blk=n_blk, past=past),
        grid=(n,),
        in_specs=[pl.BlockSpec((None, t_new, GROUP), lambda i: (i, 0, 0)),
                  pl.BlockSpec((None, n_blk, GROUP), lambda i: (i, 0, 0))],
        out_specs=pl.BlockSpec((None, rows, past), lambda i: (i, 0, 0)),
        out_shape=jax.ShapeDtypeStruct((n, rows, past), F32),
        compiler_params=_params(("parallel",)),
        name="moba_sample_gate",
    )(p3, means)


def _suffix_kernel(lf_ref, d_ref):
    ch = 256
    r, past = lf_ref.shape
    row = lax.broadcasted_iota(jnp.int32, (ch, ch), 0)
    col = lax.broadcasted_iota(jnp.int32, (ch, ch), 1)
    tri = (row > col).astype(F32)
    nch = past // ch

    def body(t, carry):
        cols = pl.ds(pl.multiple_of((nch - 1 - t) * ch, ch), ch)
        blk = lf_ref[:, cols]
        d_ref[:, cols] = _dot(blk, tri, True) + carry
        return carry + jnp.sum(blk, axis=1, keepdims=True)

    lax.fori_loop(0, nch, body, jnp.zeros((r, 1), F32))


def _fox_suffix(lft):
    r, past = lft.shape
    rb = min(r, 64)
    return pl.pallas_call(
        _suffix_kernel,
        grid=(r // rb,),
        in_specs=[pl.BlockSpec((rb, past), lambda i: (i, 0))],
        out_specs=pl.BlockSpec((rb, past), lambda i: (i, 0)),
        out_shape=jax.ShapeDtypeStruct((r, past), F32),
        compiler_params=_params(("parallel",)),
        name="fox_sample_suffix",
    )(lft)


def _decode_kernel(pt_ref, *refs, mode, g_pages, t_new):
    k_refs = refs[:g_pages]
    v_refs = refs[g_pages:2 * g_pages]
    q_ref, bias_ref, lfn_ref, kn_ref, vn_ref, y_ref, qbd_ref, m_ref, l_ref, acc_ref, pad_ref = refs[2 * g_pages:]
    c = pl.program_id(1)
    n_heads = GROUP // HEAD_DIM
    rows = t_new * n_heads
    page = PAGE_ROWS

    @pl.when(c == 0)
    def _():
        qbd_ref[...] = _build_qbd(q_ref[...], t_new)
        m_ref[...] = jnp.full(m_ref.shape, NEG, F32)
        l_ref[...] = jnp.zeros(l_ref.shape, F32)
        acc_ref[...] = jnp.zeros(acc_ref.shape, F32)

    if mode == "fox":
        lane = lax.broadcasted_iota(jnp.int32, (LANES, LANES), 1)
        srow = lax.broadcasted_iota(jnp.int32, (LANES, LANES), 0)
        cn = _dot(lfn_ref[...], (srow <= lane).astype(F32), True)

    def update(s, v_list):
        m_prev = m_ref[...]
        m_new = jnp.maximum(m_prev, jnp.max(s, axis=1, keepdims=True))
        p = jnp.exp(s - jnp.tile(m_new, (1, s.shape[1] // LANES)))
        alpha = jnp.exp(m_prev - m_new)
        l_ref[...] = alpha * l_ref[...] + jnp.sum(p, axis=1, keepdims=True)
        pv = jnp.zeros((rows, GROUP), F32)
        for g, v in enumerate(v_list):
            pv = pv + _dot(p[:, g * page:(g + 1) * page], v, True)
        acc_ref[...] = jnp.tile(alpha, (1, GROUP // LANES)) * acc_ref[...] + pv
        m_ref[...] = m_new

    qbd = qbd_ref[...]
    s = jnp.concatenate([_dot_nt(qbd, k_refs[g][...], True) for g in range(g_pages)], axis=1)
    if mode == "fox":
        bias = jnp.concatenate([bias_ref[...] + cn[:, qi:qi + 1] for qi in range(t_new)], axis=0)
    else:
        bias = bias_ref[...]
    update(s + bias, [v_refs[g][...] for g in range(g_pages)])

    @pl.when(c == pl.num_programs(1) - 1)
    def _():
        lane = lax.broadcasted_iota(jnp.int32, (n_heads, LANES), 1)
        pad_ref[...] = jnp.zeros(pad_ref.shape, F32)
        pad_ref[0, 0:t_new, :] = kn_ref[...]
        pad_ref[1, 0:t_new, :] = vn_ref[...]
        s_new = _dot_nt(qbd, pad_ref[0], True)
        if mode == "fox":
            b_new = [jnp.where(lane <= qi, cn[:, qi:qi + 1] - cn, NEG) for qi in range(t_new)]
        else:
            b_new = [jnp.where(lane <= qi, 0.0, NEG) for qi in range(t_new)]
        update(s_new + jnp.concatenate(b_new, axis=0), [pad_ref[1]])
        o = acc_ref[...] / jnp.tile(l_ref[...], (1, GROUP // LANES))
        lane_head = lax.broadcasted_iota(jnp.int32, (rows, GROUP), 1) // HEAD_DIM
        row_head = lax.broadcasted_iota(jnp.int32, (rows, GROUP), 0) % n_heads
        om = jnp.where(lane_head == row_head, o, 0.0)
        for qi in range(t_new):
            y_ref[qi:qi + 1, :] = jnp.sum(om[qi * n_heads:(qi + 1) * n_heads, :], axis=0, keepdims=True)


def _decode(cache_k, cache_v, page_table, layer, p3, bias, lfn, *, mode, qg, kg, vg, g_pages=8):
    n, n_pages = page_table.shape
    t_new = p3.shape[1]
    n_heads = GROUP // HEAD_DIM
    rows = t_new * n_heads
    brows = bias.shape[1]
    ck = g_pages * PAGE_ROWS
    in_specs = _page_specs(layer, g_pages) + _page_specs(layer, g_pages) + [
        pl.BlockSpec((None, t_new, GROUP), lambda i, c, pt: (i, 0, qg)),
        pl.BlockSpec((None, brows, ck), lambda i, c, pt: (i, 0, c)),
        pl.BlockSpec((None, n_heads, LANES), lambda i, c, pt: (i, 0, 0)),
        pl.BlockSpec((None, t_new, GROUP), lambda i, c, pt: (i, 0, kg)),
        pl.BlockSpec((None, t_new, GROUP), lambda i, c, pt: (i, 0, vg)),
    ]
    grid_spec = pltpu.PrefetchScalarGridSpec(
        num_scalar_prefetch=1,
        grid=(n, n_pages // g_pages),
        in_specs=in_specs,
        out_specs=pl.BlockSpec((None, t_new, GROUP), lambda i, c, pt: (i, 0, 0)),
        scratch_shapes=[pltpu.VMEM((rows, GROUP), F32), pltpu.VMEM((rows, LANES), F32),
                        pltpu.VMEM((rows, LANES), F32), pltpu.VMEM((rows, GROUP), F32),
                        pltpu.VMEM((2, LANES, GROUP), F32)],
    )
    return pl.pallas_call(
        functools.partial(_decode_kernel, mode=mode, g_pages=g_pages, t_new=t_new),
        grid_spec=grid_spec,
        out_shape=jax.ShapeDtypeStruct((n, t_new, GROUP), F32),
        compiler_params=_params(("parallel", "arbitrary")),
        name=mode + "_sample",
    )(page_table, *([cache_k] * g_pages), *([cache_v] * g_pages), p3, bias, lfn, p3, p3)


def _rope_tables(pos):
    half = HEAD_DIM // 2
    inv = ROPE_THETA ** (-jnp.arange(half, dtype=F32) / half)
    ang = pos.astype(F32)[:, None] * inv[None, :]
    cos = jnp.cos(ang)
    sin = jnp.sin(ang)
    cos = jnp.concatenate([cos, cos, cos, cos], axis=1)
    sin = jnp.concatenate([-sin, sin, -sin, sin], axis=1)
    return cos, sin


def _split_w_in(w, dt):
    n_heads = GROUP // HEAD_DIM
    qkv = w[:, :6 * GROUP]
    wf = w[:, 6 * GROUP:6 * GROUP + n_heads]
    gates = w[:, 6 * GROUP + n_heads:]
    wmain = jnp.concatenate([qkv, gates], axis=1).astype(dt)
    wf = jnp.pad(wf, ((0, 0), (0, LANES - n_heads))).astype(dt)
    return wmain, wf


def _moe_plan(top_idx, top_w, n_exp, tm):
    n = top_idx.shape[0]
    flat_e = top_idx.reshape(-1)
    order = jnp.argsort(flat_e, stable=True)
    sorted_e = flat_e[order]
    counts = jnp.bincount(flat_e, length=n_exp)
    padded = ((counts + tm - 1) // tm) * tm
    starts = jnp.cumsum(padded) - padded
    first = jnp.cumsum(counts) - counts
    dest_sorted = starts[sorted_e] + (jnp.arange(2 * n) - first[sorted_e])
    n_rows = ((2 * n + tm - 1) // tm + n_exp) * tm
    row_token = jnp.zeros((n_rows,), jnp.int32).at[dest_sorted].set((order // TOP_K).astype(jnp.int32))
    row_gate = jnp.zeros((n_rows,), F32).at[dest_sorted].set(top_w.reshape(-1)[order])
    dest = jnp.zeros((2 * n,), jnp.int32).at[order].set(dest_sorted.astype(jnp.int32)).reshape(n, TOP_K)
    tile_start = jnp.arange(n_rows // tm) * tm
    ends = jnp.cumsum(padded)
    tile_expert = jnp.minimum(jnp.sum(tile_start[:, None] >= ends[None, :], axis=1), n_exp - 1).astype(jnp.int32)
    n_live = (ends[-1] // tm).astype(jnp.int32).reshape(1)
    return row_token, row_gate, dest, tile_expert, n_live


def kernel(x_prompt, x_sample, cache_k_a, cache_v_a, cache_k_b, cache_v_b, cache_logf_b, page_table, w_in, b_f, w_pa,
           w_pb, w_o, ln_g, ln_b, w_gate_d, w_up_d, w_down_d, w_router, w_gate_e, w_up_e, w_down_e):
    bsz, seq, d = x_prompt.shape
    nseq, t_new, _ = x_sample.shape
    depth = w_in.shape[0]
    n_pool, page = cache_k_a.shape[1], cache_k_a.shape[2]
    n_heads = GROUP // HEAD_DIM
    n_pages = page_table.shape[1]
    past = n_pages * page
    n_exp = w_router.shape[-1]
    alpha = (2 * depth) ** 0.25
    assert past % MOBA_BLOCK == 0 and page == PAGE_ROWS and cache_k_a.shape[3] * cache_k_a.shape[4] == GROUP
    n_p = bsz * seq
    n_s = nseq * t_new
    tm = 512

    xp = x_prompt.reshape(n_p, d)
    xs = x_sample.reshape(n_s, d)
    cos_p, sin_p = _rope_tables(jnp.arange(seq))
    cos_s, sin_s = _rope_tables(jnp.tile(past + jnp.arange(t_new), nseq))
    ck_a = cache_k_a.reshape(depth, n_pool, page, GROUP)
    cv_a = cache_v_a.reshape(depth, n_pool, page, GROUP)
    ck_b = cache_k_b.reshape(depth, n_pool, page, GROUP)
    cv_b = cache_v_b.reshape(depth, n_pool, page, GROUP)
    ones_p = jnp.ones((n_p, 1), F32)
    ones_s = jnp.ones((n_s, 1), F32)
    zero_lfn = jnp.zeros((nseq, n_heads, LANES), F32)

    outs_p = [[] for _ in range(5)]
    outs_s = [[] for _ in range(5)]
    for l in range(depth):
        bfp = jnp.pad(b_f[l], (0, LANES - n_heads)).reshape(1, LANES)
        g0, b0 = ln_g[l, 0].reshape(1, d), ln_b[l, 0].reshape(1, d)
        g1, b1 = ln_g[l, 1].reshape(1, d), ln_b[l, 1].reshape(1, d)
        moe = l % 2 == 1
        wr = jnp.pad(w_router[l // 2], ((0, 0), (0, LANES - n_exp))) if moe else None

        wmain, wf = _split_w_in(w_in[l], BF16)
        pp, lfp = _project(xp, wmain, wf, bfp, cos_p, sin_p, tm=tm, table_tiles=seq // tm, hp=False)
        cp = _cumsum(lfp.reshape(bsz, seq, LANES))
        pp3 = pp.reshape(bsz, seq, N_GROUPS * GROUP)
        ya = _attn_prompt(pp3, None, mode="moba", qg=0, kg=1, vg=2).reshape(n_p, GROUP)
        yb = _attn_prompt(pp3, cp, mode="fox", qg=3, kg=4, vg=5).reshape(n_p, GROUP)
        res_p = _merge(ya, yb, pp, xp, w_pa[l].astype(BF16), w_pb[l].astype(BF16), w_o[l].astype(BF16), g0, b0, wr,
                       tm=tm, hp=False, alpha=alpha, n_exp=n_exp if moe else 0)

        wmain_s, wf_s = _split_w_in(w_in[l], F32)
        ps, lfs = _project(xs, wmain_s, wf_s, bfp, cos_s, sin_s, tm=n_s, table_tiles=1, hp=True)
        ps3 = ps.reshape(nseq, t_new, N_GROUPS * GROUP)
        means = _moba_means(ck_a, page_table, l)
        bias_a = _moba_gate(ps3[:, :, :GROUP], means, past=past)
        ya_s = _decode(ck_a, cv_a, page_table, l, ps3, bias_a, zero_lfn, mode="moba", qg=0, kg=1, vg=2)
        lf_past = jnp.take(cache_logf_b[l], page_table, axis=0).reshape(nseq, past, n_heads)
        d_past = _fox_suffix(jnp.transpose(lf_past, (0, 2, 1)).reshape(nseq * n_heads, past))
        lfn = jnp.transpose(lfs[:, :n_heads].reshape(nseq, t_new, n_heads), (0, 2, 1))
        lfn = jnp.pad(lfn, ((0, 0), (0, 0), (0, LANES - t_new)))
        yb_s = _decode(ck_b, cv_b, page_table, l, ps3, d_past.reshape(nseq, n_heads, past), lfn,
                       mode="fox", qg=3, kg=4, vg=5)
        res_s = _merge(ya_s.reshape(n_s, GROUP), yb_s.reshape(n_s, GROUP), ps, xs, w_pa[l], w_pb[l], w_o[l], g0, b0,
                       wr, tm=n_s, hp=True, alpha=alpha, n_exp=n_exp if moe else 0)

        if not moe:
            zero_te = jnp.zeros((n_p // tm,), jnp.int32)
            wg, wu, wd = w_gate_d[l // 2][None], w_up_d[l // 2][None], w_down_d[l // 2][None]
            xp = _ffn(res_p[0], ones_p, zero_te, jnp.full((1,), n_p // tm, jnp.int32),
                      wg.astype(BF16), wu.astype(BF16), wd.astype(BF16), g1, b1,
                      tm=tm, tf=512, hp=False, alpha=alpha, fuse_ln=True)
            xs = _ffn(res_s[0], ones_s, zero_te[:1], jnp.ones((1,), jnp.int32), wg, wu, wd, g1, b1,
                      tm=n_s, tf=512, hp=True, alpha=alpha, fuse_ln=True)
        else:
            x1 = jnp.concatenate([res_p[0], res_s[0]], axis=0)
            top_idx = jnp.concatenate([res_p[1], res_s[1]], axis=0)[:, :TOP_K]
            top_w = jnp.concatenate([res_p[2], res_s[2]], axis=0)[:, :TOP_K]
            row_token, row_gate, dest, tile_expert, n_live = _moe_plan(top_idx, top_w, n_exp, tm)
            xs_rows = jnp.take(x1, row_token, axis=0)
            ys = _ffn(xs_rows, row_gate[:, None], tile_expert, n_live,
                      w_gate_e[l // 2].astype(BF16), w_up_e[l // 2].astype(BF16), w_down_e[l // 2].astype(BF16),
                      g1, b1, tm=tm, tf=512, hp=False, alpha=alpha, fuse_ln=False)
            y1 = jnp.take(ys, dest[:, 0], axis=0)
            y2 = jnp.take(ys, dest[:, 1], axis=0)
            xp = _combine(x1[:n_p], y1[:n_p], y2[:n_p], g1, b1, tm=tm, alpha=alpha)
            xs = _combine(x1[n_p:], y1[n_p:], y2[n_p:], g1, b1, tm=n_s, alpha=alpha)

        for j, grp in enumerate((1, 2, 4, 5)):
            outs_p[j].append(pp[:, grp * GROUP:(grp + 1) * GROUP].reshape(bsz, seq, n_heads, HEAD_DIM))
            outs_s[j].append(ps[:, grp * GROUP:(grp + 1) * GROUP].reshape(nseq, t_new, n_heads, HEAD_DIM))
        outs_p[4].append(lfp[:, :n_heads].reshape(bsz, seq, n_heads))
        outs_s[4].append(lfs[:, :n_heads].reshape(nseq, t_new, n_heads))

    return (xp.reshape(bsz, seq, d), xs.reshape(nseq, t_new, d),
            *[jnp.stack(o) for o in outs_p], *[jnp.stack(o) for o in outs_s])
```

```python
import functools

import jax
import jax.numpy as jnp
from jax import lax
from jax.experimental import pallas as pl
from jax.experimental.pallas import tpu as pltpu

HEAD_DIM = 64
MOBA_BLOCK = 256
MOBA_TOPK = 3
TOP_K = 2
ROPE_THETA = 10000.0
LN_EPS = 1e-5
NEG = -1e30
LANES = 128
GROUP = 512
N_GROUPS = 10
PAGE_ROWS = 128
VMEM_LIMIT = 56 * 1024 * 1024
HI = lax.Precision.HIGHEST
F32 = jnp.float32
BF16 = jnp.bfloat16


def _dot(a, b, hp):
    if hp:
        return jnp.dot(a, b, precision=HI, preferred_element_type=F32)
    return jnp.dot(a, b, preferred_element_type=F32)


def _dot_nt(a, b, hp):
    dn = (((1,), (1,)), ((), ()))
    if hp:
        return lax.dot_general(a, b, dn, precision=HI, preferred_element_type=F32)
    return lax.dot_general(a, b, dn, preferred_element_type=F32)


def _split_hi_lo(a):
    hi = a.astype(BF16)
    lo = (a - hi.astype(F32)).astype(BF16)
    return hi, lo


def _stack_hi_lo(a):
    hi, lo = _split_hi_lo(a)
    return jnp.concatenate([hi, lo], axis=0)


def _dot3(a2, b, *, nt):
    r = a2.shape[0] // 2
    b_hi, b_lo = _split_hi_lo(b)
    f = _dot_nt if nt else _dot
    s2 = f(a2, b_hi, False)
    return s2[:r] + s2[r:] + f(a2[:r], b_lo, False)


def _params(sem):
    return pltpu.CompilerParams(dimension_semantics=sem, vmem_limit_bytes=VMEM_LIMIT)


def _layer_norm(z, g, b):
    mu = jnp.mean(z, axis=-1, keepdims=True)
    zc = z - mu
    var = jnp.mean(zc * zc, axis=-1, keepdims=True)
    return zc * lax.rsqrt(var + LN_EPS) * g + b


def _log_sigmoid(z):
    return jnp.minimum(z, 0.0) - jnp.log1p(jnp.exp(-jnp.abs(z)))


def _rope_cols(r, cos, sin):
    lane = lax.broadcasted_iota(jnp.int32, cos.shape, 1)
    first = (lane % HEAD_DIM) < (HEAD_DIM // 2)
    out = []
    for c in range(GROUP // LANES):
        seg = r[:, c * LANES:(c + 1) * LANES]
        rot = jnp.where(first, pltpu.roll(seg, LANES - HEAD_DIM // 2, 1), pltpu.roll(seg, HEAD_DIM // 2, 1))
        out.append(seg * cos + rot * sin)
    return jnp.concatenate(out, axis=1)


def _proj_prompt_kernel(x_ref, w_ref, wf_ref, bf_ref, cos_ref, sin_ref, *refs):
    q_ref, ka_ref, va_ref, kb_ref, vb_ref, g_ref, lf_ref = refs[-7:]
    scale = HEAD_DIM ** -0.5
    xb = x_ref[...].astype(BF16)
    cos, sin = cos_ref[...], sin_ref[...]

    def col(g):
        return _dot(xb, w_ref[:, g * GROUP:(g + 1) * GROUP], False)

    lf_ref[...] = _log_sigmoid(_dot(xb, wf_ref[...], False) + bf_ref[...])
    q_ref[:, 0:GROUP] = (_rope_cols(col(0), cos, sin) * scale).astype(BF16)
    ka_ref[...] = _rope_cols(col(1), cos, sin)
    va_ref[...] = col(2)
    q_ref[:, GROUP:2 * GROUP] = (col(3) * scale).astype(BF16)
    kb_ref[...] = col(4)
    vb_ref[...] = col(5)
    for g in range(6, N_GROUPS):
        g_ref[:, (g - 6) * GROUP:(g - 5) * GROUP] = jax.nn.sigmoid(col(g)).astype(BF16)


def _project_prompt(x, w, wf, bfp, cos, sin, state, layer, *, tm, table_tiles):
    n, d = x.shape
    depth = state["depth"]
    row = lambda width: pl.BlockSpec((tm, width), lambda i: (i, 0))
    slab = lambda width: pl.BlockSpec((None, tm, width), lambda i: (layer, i, 0))
    in_specs = [
        row(d),
        pl.BlockSpec((d, N_GROUPS * GROUP), lambda i: (0, 0), pipeline_mode=pl.Buffered(1)),
        pl.BlockSpec((d, LANES), lambda i: (0, 0)),
        pl.BlockSpec((1, LANES), lambda i: (0, 0)),
        pl.BlockSpec((tm, LANES), lambda i: (i % table_tiles, 0)),
        pl.BlockSpec((tm, LANES), lambda i: (i % table_tiles, 0)),
    ]
    args = [x, w, wf, bfp, cos, sin]
    aliases = {}
    if state["bufs"] is not None:
        in_specs += [pl.BlockSpec(memory_space=pl.ANY)] * 5
        args += list(state["bufs"])
        aliases = {6 + k: 1 + k for k in range(4)}
        aliases[10] = 6
    out = pl.pallas_call(
        _proj_prompt_kernel,
        grid=(n // tm,),
        in_specs=in_specs,
        out_specs=[row(2 * GROUP), slab(GROUP), slab(GROUP), slab(GROUP), slab(GROUP), row(4 * GROUP), slab(LANES)],
        out_shape=[jax.ShapeDtypeStruct((n, 2 * GROUP), BF16)]
        + [jax.ShapeDtypeStruct((depth, n, GROUP), F32)] * 4
        + [jax.ShapeDtypeStruct((n, 4 * GROUP), BF16), jax.ShapeDtypeStruct((depth, n, LANES), F32)],
        input_output_aliases=aliases,
        compiler_params=_params(("parallel",)),
        name="project_prompt",
    )(*args)
    q, ka, va, kb, vb, gates, lf = out
    state["bufs"] = (ka, va, kb, vb, lf)
    return q, gates


def _proj_kernel(x_ref, w_ref, wf_ref, bf_ref, cos_ref, sin_ref, p_ref, lf_ref):
    j = pl.program_id(1)
    scale = HEAD_DIM ** -0.5
    x = x_ref[...]

    @pl.when(j == 0)
    def _():
        lf_ref[...] = _log_sigmoid(_dot(x, wf_ref[...], True) + bf_ref[...])

    r = _dot(x, w_ref[...], True)

    @pl.when(j == 0)
    def _():
        p_ref[...] = _rope_cols(r, cos_ref[...], sin_ref[...]) * scale

    @pl.when(j == 1)
    def _():
        p_ref[...] = _rope_cols(r, cos_ref[...], sin_ref[...])

    @pl.when(j == 3)
    def _():
        p_ref[...] = r * scale

    @pl.when((j == 2) | (j == 4) | (j == 5))
    def _():
        p_ref[...] = r

    @pl.when(j >= 6)
    def _():
        p_ref[...] = jax.nn.sigmoid(r)


def _project_sample(x, w, wf, bfp, cos, sin):
    n, d = x.shape
    return pl.pallas_call(
        _proj_kernel,
        grid=(1, N_GROUPS),
        in_specs=[
            pl.BlockSpec((n, d), lambda i, j: (0, 0)),
            pl.BlockSpec((d, GROUP), lambda i, j: (0, j)),
            pl.BlockSpec((d, LANES), lambda i, j: (0, 0)),
            pl.BlockSpec((1, LANES), lambda i, j: (0, 0)),
            pl.BlockSpec((n, LANES), lambda i, j: (0, 0)),
            pl.BlockSpec((n, LANES), lambda i, j: (0, 0)),
        ],
        out_specs=[
            pl.BlockSpec((n, GROUP), lambda i, j: (0, j)),
            pl.BlockSpec((n, LANES), lambda i, j: (0, 0)),
        ],
        out_shape=[jax.ShapeDtypeStruct((n, N_GROUPS * GROUP), F32),
                   jax.ShapeDtypeStruct((n, LANES), F32)],
        compiler_params=_params(("parallel", "arbitrary")),
        name="project_sample",
    )(x, w, wf, bfp, cos, sin)


def _cumsum_kernel(lf_ref, c_ref):
    ch = 256
    s_len = lf_ref.shape[0]
    row = lax.broadcasted_iota(jnp.int32, (ch, ch), 0)
    col = lax.broadcasted_iota(jnp.int32, (ch, ch), 1)
    tri = (col <= row).astype(F32)

    def body(k, carry):
        rows = pl.ds(pl.multiple_of(k * ch, ch), ch)
        c = _dot(tri, lf_ref[rows, :], True) + carry
        c_ref[rows, :] = c
        return c[ch - 1:ch, :]

    lax.fori_loop(0, s_len // ch, body, jnp.zeros((1, LANES), F32))


def _cumsum(lf4, layer):
    _, b, s, _ = lf4.shape
    return pl.pallas_call(
        _cumsum_kernel,
        grid=(b,),
        in_specs=[pl.BlockSpec((None, None, s, LANES), lambda i: (layer, i, 0, 0))],
        out_specs=pl.BlockSpec((None, s, LANES), lambda i: (i, 0, 0)),
        out_shape=jax.ShapeDtypeStruct((b, s, LANES), F32),
        compiler_params=_params(("parallel",)),
        name="fox_cumsum",
    )(lf4)


def _split3(c):
    hi = c.astype(BF16).astype(F32)
    mid = (c - hi).astype(BF16).astype(F32)
    lo = c - hi - mid
    return hi, mid, lo


def _place(lane, base, vals):
    out = jnp.zeros(lane.shape, F32)
    for k, v in enumerate(vals):
        out = jnp.where(lane == base + k, v, out)
    return out


def _attn_prompt_kernel(*refs, mode, tq, tk, n_blk):
    if mode == "fox":
        q_ref, k_ref, v_ref, c_ref, y_ref, kx_ref, vx_ref, qx_ref, m_ref, acc_ref = refs
    else:
        q_ref, k_ref, v_ref, y_ref, kx_ref, vx_ref, qx_ref, m_ref, acc_ref, mean_ref = refs
    hp = pl.program_id(1)
    i = pl.program_id(2)
    s_len = k_ref.shape[0]
    pc = 512
    bases = (HEAD_DIM, 0)

    @pl.when(i == 0)
    def _():
        def body(t, _):
            rows = pl.ds(pl.multiple_of(t * pc, pc), pc)
            kc = k_ref[rows, :]
            vc = v_ref[rows, :]
            lane = lax.broadcasted_iota(jnp.int32, (pc, LANES), 1)
            if mode == "moba":
                for u in range(pc // MOBA_BLOCK):
                    mean = jnp.sum(kc[u * MOBA_BLOCK:(u + 1) * MOBA_BLOCK, :], axis=0, keepdims=True)
                    mean_ref[pl.ds(t * (pc // MOBA_BLOCK) + u, 1), :] = mean * (1.0 / MOBA_BLOCK)
            for head in range(2):
                base = bases[head]
                own = lane // HEAD_DIM == head
                if mode == "fox":
                    h = 2 * hp + head
                    ck = jnp.sum(jnp.where(lane == h, c_ref[rows, :], 0.0), axis=1, keepdims=True)
                    hi, mid, lo = _split3(ck)
                    ex = _place(lane, base, [1.0, 1.0, 1.0, -hi, -mid, -lo])
                else:
                    blk = (t * pc + lax.broadcasted_iota(jnp.int32, (pc, LANES), 0)) // MOBA_BLOCK
                    ex = jnp.where(lane - base == blk, 1.0, 0.0)
                kx_ref[head, rows, :] = jnp.where(own, kc, ex).astype(BF16)
                vx_ref[head, rows, :] = jnp.where(own, vc, 1.0).astype(BF16)
            return 0

        lax.fori_loop(0, s_len // pc, body, 0)

    qq = q_ref[...].astype(F32)
    lane = lax.broadcasted_iota(jnp.int32, (tq, LANES), 1)
    for head in range(2):
        base = bases[head]
        own = lane // HEAD_DIM == head
        if mode == "fox":
            h = 2 * hp + head
            cq = jnp.sum(jnp.where(lane == h, c_ref[pl.ds(pl.multiple_of(i * tq, tq), tq), :], 0.0),
                         axis=1, keepdims=True)
            hi, mid, lo = _split3(cq)
            ex = _place(lane, base, [hi, mid, lo, 1.0, 1.0, 1.0])
        else:
            gate_t = _dot_nt(mean_ref[...], jnp.where(own, qq, 0.0), True)
            nidx = lax.broadcasted_iota(jnp.int32, (n_blk, tq), 0)
            cb = (i * tq + lax.broadcasted_iota(jnp.int32, (1, tq), 1)) // MOBA_BLOCK
            rank = jnp.zeros((n_blk, tq), F32)
            for n2 in range(n_blk):
                g2 = gate_t[n2:n2 + 1, :]
                beats = (g2 > gate_t) | ((g2 == gate_t) & (n2 < nidx))
                rank = rank + jnp.where(beats & (n2 < cb), 1.0, 0.0)
            keep = ((nidx < cb) & (rank < MOBA_TOPK)) | (nidx == cb)
            pen_t = jnp.where(keep, 0.0, NEG)
            pieces = [pen_t, jnp.full((LANES - n_blk, tq), NEG, F32)]
            if base:
                pieces = [jnp.full((base, tq), NEG, F32), pen_t, jnp.full((LANES - base - n_blk, tq), NEG, F32)]
            ex = jnp.transpose(jnp.concatenate(pieces, axis=0))
        qx_ref[head] = jnp.where(own, qq, ex).astype(BF16)
    m_ref[...] = jnp.full(m_ref.shape, NEG, F32)
    acc_ref[...] = jnp.zeros(acc_ref.shape, F32)

    def chunk(j, masked):
        rows = pl.ds(pl.multiple_of(j * tk, tk), tk)
        for head in range(2):
            s = _dot_nt(qx_ref[head], kx_ref[head, rows, :], False)
            if masked:
                kpos = j * tk + lax.broadcasted_iota(jnp.int32, (tq, tk), 1)
                qp = i * tq + lax.broadcasted_iota(jnp.int32, (tq, tk), 0)
                s = jnp.where(kpos <= qp, s, NEG)
            m_prev = m_ref[head]
            m_new = jnp.maximum(m_prev, jnp.max(s, axis=1, keepdims=True))
            p = jnp.exp(s - jnp.tile(m_new, (1, tk // LANES)))
            acc_ref[head] = (jnp.exp(m_prev - m_new) * acc_ref[head]
                             + _dot(p.astype(BF16), vx_ref[head, rows, :], False))
            m_ref[head] = m_new

    per_tile = tq // tk
    n_full = i * per_tile

    def body(t, _):
        for d in range(per_tile):
            chunk(t * per_tile + d, False)
        return 0

    lax.fori_loop(0, i, body, 0)
    for d in range(per_tile):
        chunk(n_full + d, True)

    first = lane < HEAD_DIM
    a0, a1 = acc_ref[0], acc_ref[1]
    num = jnp.where(first, a0, a1)
    den = jnp.where(first, pltpu.roll(a0, HEAD_DIM, 1), pltpu.roll(a1, HEAD_DIM, 1))
    y_ref[...] = (num / den).astype(y_ref.dtype)


def _attn_prompt(q3, k4, v4, c3, layer, *, mode, qoff, tq=512, tk=256):
    b, s, _ = q3.shape
    pairs = GROUP // LANES
    n_blk = s // MOBA_BLOCK
    assert s % tq == 0 and tq % tk == 0 and tk == MOBA_BLOCK and n_blk <= HEAD_DIM and n_blk % 8 == 0
    kv_spec = pl.BlockSpec((None, None, s, LANES), lambda bi, h, i: (layer, bi, 0, h))
    in_specs = [pl.BlockSpec((None, tq, LANES), lambda bi, h, i: (bi, i, qoff * pairs + h)), kv_spec, kv_spec]
    args = [q3, k4, v4]
    scratch = [pltpu.VMEM((2, s, LANES), BF16), pltpu.VMEM((2, s, LANES), BF16), pltpu.VMEM((2, tq, LANES), BF16),
               pltpu.VMEM((2, tq, LANES), F32), pltpu.VMEM((2, tq, LANES), F32)]
    if mode == "fox":
        in_specs.append(pl.BlockSpec((None, s, LANES), lambda bi, h, i: (bi, 0, 0)))
        args.append(c3)
    else:
        scratch.append(pltpu.VMEM((n_blk, LANES), F32))
    return pl.pallas_call(
        functools.partial(_attn_prompt_kernel, mode=mode, tq=tq, tk=tk, n_blk=n_blk),
        grid=(b, pairs, s // tq),
        in_specs=in_specs,
        out_specs=pl.BlockSpec((None, tq, LANES), lambda bi, h, i: (bi, i, h)),
        out_shape=jax.ShapeDtypeStruct((b, s, GROUP), BF16),
        scratch_shapes=scratch,
        compiler_params=_params(("parallel", "parallel", "arbitrary")),
        name=mode + "_prompt",
    )(*args)


def _merge_kernel(*refs, hp, alpha, n_exp):
    if n_exp:
        (ya_ref, yb_ref, ga_ref, gb_ref, x_ref, wpa_ref, wpb_ref, wo_ref, g_ref, b_ref, wr_ref,
         o_ref, idx_ref, wt_ref) = refs
    else:
        ya_ref, yb_ref, ga_ref, gb_ref, x_ref, wpa_ref, wpb_ref, wo_ref, g_ref, b_ref, o_ref = refs
    cdt = F32 if hp else BF16
    m = (ga_ref[...].astype(F32) * _dot(ya_ref[...], wpa_ref[...], hp)
         + gb_ref[...].astype(F32) * _dot(yb_ref[...], wpb_ref[...], hp))
    h = _dot(m.astype(cdt), wo_ref[...], hp)
    x1 = _layer_norm(alpha * x_ref[...] + h, g_ref[...], b_ref[...])
    o_ref[...] = x1
    if n_exp:
        logits = _dot(x1, wr_ref[...], True) if hp else _dot3(_stack_hi_lo(x1), wr_ref[...], nt=False)
        lane = lax.broadcasted_iota(jnp.int32, logits.shape, 1)
        logits = jnp.where(lane < n_exp, logits, NEG)
        v1 = jnp.max(logits, axis=1, keepdims=True)
        i1 = jnp.min(jnp.where(logits == v1, lane, LANES), axis=1, keepdims=True)
        rest = jnp.where(lane == i1, NEG, logits)
        v2 = jnp.max(rest, axis=1, keepdims=True)
        i2 = jnp.min(jnp.where(rest == v2, lane, LANES), axis=1, keepdims=True)
        e = jnp.exp(v2 - v1)
        w1 = 1.0 / (1.0 + e)
        w2 = e / (1.0 + e)
        idx_ref[...] = jnp.where(lane == 0, i1, jnp.where(lane == 1, i2, 0))
        wt_ref[...] = jnp.where(lane == 0, w1, jnp.where(lane == 1, w2, 0.0))


def _merge(ya, yb, gates, ga_blk, gb_blk, x, wpa, wpb, wo, g, b, wr, *, tm, hp, alpha, n_exp):
    n, d = x.shape
    assert d == 2 * GROUP
    in_specs = [
        pl.BlockSpec((tm, GROUP), lambda i: (i, 0)),
        pl.BlockSpec((tm, GROUP), lambda i: (i, 0)),
        pl.BlockSpec((tm, d), lambda i: (i, ga_blk)),
        pl.BlockSpec((tm, d), lambda i: (i, gb_blk)),
        pl.BlockSpec((tm, d), lambda i: (i, 0)),
        pl.BlockSpec((GROUP, d), lambda i: (0, 0)),
        pl.BlockSpec((GROUP, d), lambda i: (0, 0)),
        pl.BlockSpec((d, d), lambda i: (0, 0)),
        pl.BlockSpec((1, d), lambda i: (0, 0)),
        pl.BlockSpec((1, d), lambda i: (0, 0)),
    ]
    args = [ya, yb, gates, gates, x, wpa, wpb, wo, g, b]
    out_specs = [pl.BlockSpec((tm, d), lambda i: (i, 0))]
    out_shape = [jax.ShapeDtypeStruct((n, d), F32)]
    if n_exp:
        in_specs.append(pl.BlockSpec((d, LANES), lambda i: (0, 0)))
        args.append(wr)
        out_specs += [pl.BlockSpec((tm, LANES), lambda i: (i, 0))] * 2
        out_shape += [jax.ShapeDtypeStruct((n, LANES), jnp.int32), jax.ShapeDtypeStruct((n, LANES), F32)]
    return pl.pallas_call(
        functools.partial(_merge_kernel, hp=hp, alpha=alpha, n_exp=n_exp),
        grid=(n // tm,),
        in_specs=in_specs,
        out_specs=out_specs,
        out_shape=out_shape,
        compiler_params=_params(("parallel",)),
        name="merge_out",
    )(*args)


def _ffn_kernel(te_ref, nv_ref, x_ref, sc_ref, wg_ref, wu_ref, wd_ref, g_ref, b_ref, o_ref, xb_ref, acc_ref,
                *, hp, alpha, fuse_ln):
    i = pl.program_id(0)
    f = pl.program_id(1)
    cdt = F32 if hp else BF16
    live = i < nv_ref[0]

    @pl.when(f == 0)
    def _():
        xb_ref[...] = x_ref[...].astype(cdt)
        acc_ref[...] = jnp.zeros(acc_ref.shape, F32)

    @pl.when(live)
    def _():
        xb = xb_ref[...]
        a = _dot(xb, wg_ref[...], hp)
        u = _dot(xb, wu_ref[...], hp)
        hmid = a * jax.nn.sigmoid(a) * u
        acc_ref[...] += _dot(hmid.astype(cdt), wd_ref[...], hp)

    @pl.when(f == pl.num_programs(1) - 1)
    def _():
        if fuse_ln:
            o_ref[...] = _layer_norm(alpha * x_ref[...] + acc_ref[...], g_ref[...], b_ref[...])
        else:
            o_ref[...] = acc_ref[...] * sc_ref[...]


def _ffn(x, scale, tile_expert, n_live, wg, wu, wd, g, b, *, tm, tf, hp, alpha, fuse_ln):
    r, d = x.shape
    _, _, dff = wg.shape
    grid_spec = pltpu.PrefetchScalarGridSpec(
        num_scalar_prefetch=2,
        grid=(r // tm, dff // tf),
        in_specs=[
            pl.BlockSpec((tm, d), lambda i, f, te, nv: (i, 0)),
            pl.BlockSpec((tm, 1), lambda i, f, te, nv: (i, 0)),
            pl.BlockSpec((None, d, tf), lambda i, f, te, nv: (te[i], 0, f)),
            pl.BlockSpec((None, d, tf), lambda i, f, te, nv: (te[i], 0, f)),
            pl.BlockSpec((None, tf, d), lambda i, f, te, nv: (te[i], f, 0)),
            pl.BlockSpec((1, d), lambda i, f, te, nv: (0, 0)),
            pl.BlockSpec((1, d), lambda i, f, te, nv: (0, 0)),
        ],
        out_specs=pl.BlockSpec((tm, d), lambda i, f, te, nv: (i, 0)),
        scratch_shapes=[pltpu.VMEM((tm, d), F32 if hp else BF16), pltpu.VMEM((tm, d), F32)],
    )
    return pl.pallas_call(
        functools.partial(_ffn_kernel, hp=hp, alpha=alpha, fuse_ln=fuse_ln),
        grid_spec=grid_spec,
        out_shape=jax.ShapeDtypeStruct((r, d), F32),
        compiler_params=_params(("parallel", "arbitrary")),
        name="swiglu",
    )(tile_expert, n_live, x, scale, wg, wu, wd, g, b)


def _combine_kernel(x_ref, y1_ref, y2_ref, g_ref, b_ref, o_ref, *, alpha):
    o_ref[...] = _layer_norm(alpha * x_ref[...] + (y1_ref[...] + y2_ref[...]), g_ref[...], b_ref[...])


def _combine(x, y1, y2, g, b, *, tm, alpha):
    n, d = x.shape
    row = pl.BlockSpec((tm, d), lambda i: (i, 0))
    vec = pl.BlockSpec((1, d), lambda i: (0, 0))
    return pl.pallas_call(
        functools.partial(_combine_kernel, alpha=alpha),
        grid=(n // tm,),
        in_specs=[row, row, row, vec, vec],
        out_specs=row,
        out_shape=jax.ShapeDtypeStruct((n, d), F32),
        compiler_params=_params(("parallel",)),
        name="moe_combine",
    )(x, y1, y2, g, b)


def _page_specs(layer, g_pages, rows, chunk_of):
    def spec(g):
        return pl.BlockSpec((None, None, rows, PAGE_ROWS),
                            lambda n, c, pt: (layer, pt[n, chunk_of(c) * g_pages + g], 0, 0))
    return [spec(g) for g in range(g_pages)]


def _build_qbd(q, t_new):
    n_heads = GROUP // HEAD_DIM
    lane_head = lax.broadcasted_iota(jnp.int32, (n_heads, GROUP), 1) // HEAD_DIM
    row_head = lax.broadcasted_iota(jnp.int32, (n_heads, GROUP), 0)
    parts = [jnp.where(lane_head == row_head, jnp.broadcast_to(q[qi:qi + 1, :], (n_heads, GROUP)), 0.0)
             for qi in range(t_new)]
    return jnp.concatenate(parts, axis=0)


def _fold_heads(o, t_new):
    n_heads = GROUP // HEAD_DIM
    rows = t_new * n_heads
    lane_head = lax.broadcasted_iota(jnp.int32, (rows, GROUP), 1) // HEAD_DIM
    row_head = lax.broadcasted_iota(jnp.int32, (rows, GROUP), 0) % n_heads
    om = jnp.where(lane_head == row_head, o, 0.0)
    return [jnp.sum(om[qi * n_heads:(qi + 1) * n_heads, :], axis=0, keepdims=True) for qi in range(t_new)]


def _new_key_tiles(pad_ref, kn_ref, vn_ref, t_new):
    pad_ref[...] = jnp.zeros(pad_ref.shape, F32)
    pad_ref[0, 0:t_new, :] = kn_ref[...]
    pad_ref[1, 0:t_new, :] = vn_ref[...]


def _fox_decode_kernel(pt_ref, *refs, g_pages, t_new):
    kt_refs = refs[:g_pages]
    vt_refs = refs[g_pages:2 * g_pages]
    lf_refs = refs[2 * g_pages:3 * g_pages]
    q_ref, lfn_ref, kn_ref, vn_ref, y_ref, q2_ref, m_ref, l_ref, acc_ref, car_ref, pad_ref = refs[3 * g_pages:]
    c = pl.program_id(1)
    n_heads = GROUP // HEAD_DIM
    rows = t_new * n_heads

    @pl.when(c == 0)
    def _():
        q2_ref[...] = _stack_hi_lo(_build_qbd(q_ref[...], t_new))
        m_ref[...] = jnp.full(m_ref.shape, NEG, F32)
        l_ref[...] = jnp.zeros(l_ref.shape, F32)
        acc_ref[...] = jnp.zeros(acc_ref.shape, F32)
        car_ref[...] = jnp.zeros(car_ref.shape, F32)

    lane = lax.broadcasted_iota(jnp.int32, (LANES, LANES), 1)
    srow = lax.broadcasted_iota(jnp.int32, (LANES, LANES), 0)
    cn = _dot(lfn_ref[...], (srow <= lane).astype(F32), True)

    def update(s, pv):
        m_prev = m_ref[...]
        m_new = jnp.maximum(m_prev, jnp.max(s, axis=1, keepdims=True))
        p = jnp.exp(s - jnp.tile(m_new, (1, s.shape[1] // LANES)))
        alpha = jnp.exp(m_prev - m_new)
        l_ref[...] = alpha * l_ref[...] + jnp.sum(p, axis=1, keepdims=True)
        acc_ref[...] = jnp.tile(alpha, (1, GROUP // LANES)) * acc_ref[...] + pv(p)
        m_ref[...] = m_new

    hl = lax.broadcasted_iota(jnp.int32, (n_heads, LANES), 1)
    car = car_ref[...]
    d_pages = [None] * g_pages
    for g in reversed(range(g_pages)):
        x = lf_refs[g][...]
        y = x
        k = 1
        while k < LANES:
            y = y + jnp.where(hl < LANES - k, pltpu.roll(y, LANES - k, 1), 0.0)
            k *= 2
        d_pages[g] = (y - x) + car
        car = car + jnp.broadcast_to(y[:, 0:1], (n_heads, LANES))
    car_ref[...] = car
    dsuf = jnp.concatenate(d_pages, axis=1)
    bias = jnp.concatenate([dsuf + cn[:, qi:qi + 1] for qi in range(t_new)], axis=0)
    q2 = q2_ref[...]
    kt = jnp.concatenate([r[...] for r in kt_refs], axis=1)
    vt = jnp.concatenate([r[...] for r in vt_refs], axis=1)
    update(_dot3(q2, kt, nt=False) + bias, lambda p: _dot3(_stack_hi_lo(p), vt, nt=True))

    @pl.when(c == pl.num_programs(1) - 1)
    def _():
        _new_key_tiles(pad_ref, kn_ref, vn_ref, t_new)
        qbd = _build_qbd(q_ref[...], t_new)
        s_new = _dot_nt(qbd, pad_ref[0], True)
        b_new = [jnp.where(hl <= qi, cn[:, qi:qi + 1] - cn, NEG) for qi in range(t_new)]
        update(s_new + jnp.concatenate(b_new, axis=0), lambda p: _dot(p, pad_ref[1], True))
        o = acc_ref[...] / jnp.tile(l_ref[...], (1, GROUP // LANES))
        for qi, r in enumerate(_fold_heads(o, t_new)):
            y_ref[qi:qi + 1, :] = r


def _fox_decode(kt, vt, lft, page_table, layer, p3, lfn, *, qg, kg, vg, g_pages=8):
    n, n_pages = page_table.shape
    t_new = p3.shape[1]
    n_heads = GROUP // HEAD_DIM
    rows = t_new * n_heads
    nc = n_pages // g_pages
    rev = lambda c: nc - 1 - c
    in_specs = (_page_specs(layer, g_pages, GROUP, rev) + _page_specs(layer, g_pages, GROUP, rev)
                + _page_specs(layer, g_pages, n_heads, rev) + [
        pl.BlockSpec((None, t_new, GROUP), lambda i, c, pt: (i, 0, qg)),
        pl.BlockSpec((None, n_heads, LANES), lambda i, c, pt: (i, 0, 0)),
        pl.BlockSpec((None, t_new, GROUP), lambda i, c, pt: (i, 0, kg)),
        pl.BlockSpec((None, t_new, GROUP), lambda i, c, pt: (i, 0, vg)),
    ])
    grid_spec = pltpu.PrefetchScalarGridSpec(
        num_scalar_prefetch=1,
        grid=(n, nc),
        in_specs=in_specs,
        out_specs=pl.BlockSpec((None, t_new, GROUP), lambda i, c, pt: (i, 0, 0)),
        scratch_shapes=[pltpu.VMEM((2 * rows, GROUP), BF16), pltpu.VMEM((rows, LANES), F32),
                        pltpu.VMEM((rows, LANES), F32), pltpu.VMEM((rows, GROUP), F32),
                        pltpu.VMEM((n_heads, LANES), F32), pltpu.VMEM((2, LANES, GROUP), F32)],
    )
    return pl.pallas_call(
        functools.partial(_fox_decode_kernel, g_pages=g_pages, t_new=t_new),
        grid_spec=grid_spec,
        out_shape=jax.ShapeDtypeStruct((n, t_new, GROUP), F32),
        compiler_params=_params(("parallel", "arbitrary")),
        name="fox_sample",
    )(page_table, *([kt] * g_pages), *([vt] * g_pages), *([lft] * g_pages), p3, lfn, p3, p3)


def _moba_decode_kernel(pt_ref, *refs, g_pages, t_new, n_blk):
    kt_refs = refs[:g_pages]
    vt_refs = refs[g_pages:2 * g_pages]
    q_ref, kn_ref, vn_ref, y_ref, q2_ref, s_ref, l_ref, acc_ref, pad_ref = refs[2 * g_pages:]
    c = pl.program_id(1)
    nc = pl.num_programs(1) // 2
    n_heads = GROUP // HEAD_DIM
    rows = t_new * n_heads
    ck = g_pages * PAGE_ROWS

    @pl.when(c == 0)
    def _():
        q2_ref[...] = _stack_hi_lo(_build_qbd(q_ref[...], t_new))

    @pl.when(c < nc)
    def _():
        kt = jnp.concatenate([r[...] for r in kt_refs], axis=1)
        s_ref[:, pl.ds(pl.multiple_of(c * ck, ck), ck)] = _dot3(q2_ref[...], kt, nt=False)

    @pl.when(c == nc - 1)
    def _():
        lane = lax.broadcasted_iota(jnp.int32, (rows, LANES), 1)

        def blk_cols(j):
            return pl.ds(pl.multiple_of(j * MOBA_BLOCK, MOBA_BLOCK), MOBA_BLOCK)

        def gate_body(j, gate):
            tot = jnp.sum(s_ref[:, blk_cols(j)], axis=1, keepdims=True)
            return jnp.where(lane == j, tot, gate)

        gate = lax.fori_loop(0, n_blk, gate_body, jnp.full((rows, LANES), NEG, F32))
        rank = jnp.zeros((rows, LANES), F32)
        for n2 in range(n_blk):
            g2 = gate[:, n2:n2 + 1]
            beats = (g2 > gate) | ((g2 == gate) & (n2 < lane))
            rank = rank + jnp.where(beats, 1.0, 0.0)
        sel = jnp.where((lane < n_blk) & (rank < MOBA_TOPK), 1.0, 0.0)

        def mask_body(j, m_run):
            hit = jnp.max(jnp.where(lane == j, sel, 0.0), axis=1, keepdims=True)
            sm = jnp.where(hit > 0.5, s_ref[:, blk_cols(j)], NEG)
            s_ref[:, blk_cols(j)] = sm
            return jnp.maximum(m_run, jnp.max(sm, axis=1, keepdims=True))

        m_past = lax.fori_loop(0, n_blk, mask_body, jnp.full((rows, LANES), NEG, F32))
        _new_key_tiles(pad_ref, kn_ref, vn_ref, t_new)
        qbd = _build_qbd(q_ref[...], t_new)
        hl = lax.broadcasted_iota(jnp.int32, (n_heads, LANES), 1)
        causal = jnp.concatenate([jnp.where(hl <= qi, 0.0, NEG) for qi in range(t_new)], axis=0)
        s_new = _dot_nt(qbd, pad_ref[0], True) + causal
        m = jnp.maximum(m_past, jnp.max(s_new, axis=1, keepdims=True))

        def exp_body(j, l_run):
            p = jnp.exp(s_ref[:, blk_cols(j)] - jnp.tile(m, (1, MOBA_BLOCK // LANES)))
            s_ref[:, blk_cols(j)] = p
            return l_run + jnp.sum(p, axis=1, keepdims=True)

        l_past = lax.fori_loop(0, n_blk, exp_body, jnp.zeros((rows, LANES), F32))
        p_new = jnp.exp(s_new - m)
        l_ref[...] = l_past + jnp.sum(p_new, axis=1, keepdims=True)
        acc_ref[...] = _dot(p_new, pad_ref[1], True)

    @pl.when(c >= nc)
    def _():
        vt = jnp.concatenate([r[...] for r in vt_refs], axis=1)
        p = s_ref[:, pl.ds(pl.multiple_of((c - nc) * ck, ck), ck)]
        acc_ref[...] += _dot3(_stack_hi_lo(p), vt, nt=True)

    @pl.when(c == 2 * nc - 1)
    def _():
        o = acc_ref[...] / jnp.tile(l_ref[...], (1, GROUP // LANES))
        for qi, r in enumerate(_fold_heads(o, t_new)):
            y_ref[qi:qi + 1, :] = r


def _moba_decode(kt, vt, page_table, layer, p3, *, qg, kg, vg, g_pages=8):
    n, n_pages = page_table.shape
    t_new = p3.shape[1]
    n_heads = GROUP // HEAD_DIM
    rows = t_new * n_heads
    nc = n_pages // g_pages
    past = n_pages * PAGE_ROWS
    n_blk = past // MOBA_BLOCK
    assert MOBA_TOPK <= n_blk <= LANES
    in_specs = (_page_specs(layer, g_pages, GROUP, lambda c: jnp.minimum(c, nc - 1))
                + _page_specs(layer, g_pages, GROUP, lambda c: jnp.maximum(c - nc, 0)) + [
        pl.BlockSpec((None, t_new, GROUP), lambda i, c, pt: (i, 0, qg)),
        pl.BlockSpec((None, t_new, GROUP), lambda i, c, pt: (i, 0, kg)),
        pl.BlockSpec((None, t_new, GROUP), lambda i, c, pt: (i, 0, vg)),
    ])
    grid_spec = pltpu.PrefetchScalarGridSpec(
        num_scalar_prefetch=1,
        grid=(n, 2 * nc),
        in_specs=in_specs,
        out_specs=pl.BlockSpec((None, t_new, GROUP), lambda i, c, pt: (i, 0, 0)),
        scratch_shapes=[pltpu.VMEM((2 * rows, GROUP), BF16), pltpu.VMEM((rows, past), F32),
                        pltpu.VMEM((rows, LANES), F32), pltpu.VMEM((rows, GROUP), F32),
                        pltpu.VMEM((2, LANES, GROUP), F32)],
    )
    return pl.pallas_call(
        functools.partial(_moba_decode_kernel, g_pages=g_pages, t_new=t_new, n_blk=n_blk),
        grid_spec=grid_spec,
        out_shape=jax.ShapeDtypeStruct((n, t_new, GROUP), F32),
        compiler_params=_params(("parallel", "arbitrary")),
        name="moba_sample",
    )(page_table, *([kt] * g_pages), *([vt] * g_pages), p3, p3, p3)


def _rope_tables(pos):
    half = HEAD_DIM // 2
    inv = ROPE_THETA ** (-jnp.arange(half, dtype=F32) / half)
    ang = pos.astype(F32)[:, None] * inv[None, :]
    cos = jnp.cos(ang)
    sin = jnp.sin(ang)
    cos = jnp.concatenate([cos, cos, cos, cos], axis=1)
    sin = jnp.concatenate([-sin, sin, -sin, sin], axis=1)
    return cos, sin


def _split_w_in(w, dt):
    n_heads = GROUP // HEAD_DIM
    qkv = w[:, :6 * GROUP]
    wf = w[:, 6 * GROUP:6 * GROUP + n_heads]
    gates = w[:, 6 * GROUP + n_heads:]
    wmain = jnp.concatenate([qkv, gates], axis=1).astype(dt)
    wf = jnp.pad(wf, ((0, 0), (0, LANES - n_heads))).astype(dt)
    return wmain, wf


def _moe_plan(top_idx, top_w, n_exp, tm):
    n = top_idx.shape[0]
    n_pairs = TOP_K * n
    flat_e = top_idx.reshape(-1)
    onehot = (flat_e[None, :] == jnp.arange(n_exp, dtype=jnp.int32)[:, None]).astype(jnp.int32)
    seen = jnp.cumsum(onehot, axis=1)
    counts = seen[:, -1]
    padded = ((counts + tm - 1) // tm) * tm
    ends = jnp.cumsum(padded)
    starts = ends - padded
    dest = jnp.sum(onehot * (seen - 1 + starts[:, None]), axis=0)
    n_tiles = (n_pairs + tm - 1) // tm + n_exp
    tile_start = jnp.arange(n_tiles, dtype=jnp.int32) * tm
    tile_expert = jnp.minimum(jnp.sum((tile_start[:, None] >= ends[None, :]).astype(jnp.int32), axis=1), n_exp - 1)
    row_pair = jnp.full((n_tiles * tm,), -1, jnp.int32).at[dest].set(jnp.arange(n_pairs, dtype=jnp.int32))
    row_live = row_pair >= 0
    row_token = jnp.where(row_live, row_pair // TOP_K, 0)
    row_gate = jnp.where(row_live, jnp.take(top_w.reshape(-1), row_pair, mode="clip"), 0.0)
    n_live = (ends[-1] // tm).astype(jnp.int32).reshape(1)
    return row_token, row_gate, dest.reshape(n, TOP_K), tile_expert.astype(jnp.int32), n_live


def kernel(x_prompt, x_sample, cache_k_a, cache_v_a, cache_k_b, cache_v_b, cache_logf_b, page_table, w_in, b_f, w_pa,
           w_pb, w_o, ln_g, ln_b, w_gate_d, w_up_d, w_down_d, w_router, w_gate_e, w_up_e, w_down_e):
    bsz, seq, d = x_prompt.shape
    nseq, t_new, _ = x_sample.shape
    depth = w_in.shape[0]
    n_pool, page = cache_k_a.shape[1], cache_k_a.shape[2]
    n_heads = GROUP // HEAD_DIM
    n_pages = page_table.shape[1]
    past = n_pages * page
    n_exp = w_router.shape[-1]
    alpha = (2 * depth) ** 0.25
    assert past % MOBA_BLOCK == 0 and page == PAGE_ROWS and cache_k_a.shape[3] * cache_k_a.shape[4] == GROUP
    n_p = bsz * seq
    n_s = nseq * t_new
    tm = 512

    xp = x_prompt.reshape(n_p, d)
    xs = x_sample.reshape(n_s, d)
    cos_p, sin_p = _rope_tables(jnp.arange(seq))
    cos_s, sin_s = _rope_tables(jnp.tile(past + jnp.arange(t_new), nseq))
    page_t = lambda c: jnp.transpose(c, (0, 1, 3, 4, 2)).reshape(depth, n_pool, GROUP, page)
    kt_a, vt_a, kt_b, vt_b = page_t(cache_k_a), page_t(cache_v_a), page_t(cache_k_b), page_t(cache_v_b)
    lft_b = jnp.transpose(cache_logf_b, (0, 1, 3, 2))
    ones_p = jnp.ones((n_p, 1), F32)
    ones_s = jnp.ones((n_s, 1), F32)

    state = {"depth": depth, "bufs": None}
    outs_s = [[] for _ in range(5)]
    for l in range(depth):
        bfp = jnp.pad(b_f[l], (0, LANES - n_heads)).reshape(1, LANES)
        g0, b0 = ln_g[l, 0].reshape(1, d), ln_b[l, 0].reshape(1, d)
        g1, b1 = ln_g[l, 1].reshape(1, d), ln_b[l, 1].reshape(1, d)
        moe = l % 2 == 1
        wr = jnp.pad(w_router[l // 2], ((0, 0), (0, LANES - n_exp))) if moe else None

        wmain, wf = _split_w_in(w_in[l], BF16)
        qp, gates_p = _project_prompt(xp, wmain, wf, bfp, cos_p, sin_p, state, l, tm=tm, table_tiles=seq // tm)
        ka, va, kb, vb, lfp = state["bufs"]
        kv4 = lambda a: a.reshape(depth, bsz, seq, GROUP)
        cp = _cumsum(lfp.reshape(depth, bsz, seq, LANES), l)
        qp3 = qp.reshape(bsz, seq, 2 * GROUP)
        ya = _attn_prompt(qp3, kv4(ka), kv4(va), None, l, mode="moba", qoff=0).reshape(n_p, GROUP)
        yb = _attn_prompt(qp3, kv4(kb), kv4(vb), cp, l, mode="fox", qoff=1).reshape(n_p, GROUP)
        res_p = _merge(ya, yb, gates_p, 0, 1, xp, w_pa[l].astype(BF16), w_pb[l].astype(BF16), w_o[l].astype(BF16),
                       g0, b0, wr, tm=tm, hp=False, alpha=alpha, n_exp=n_exp if moe else 0)

        wmain_s, wf_s = _split_w_in(w_in[l], F32)
        ps, lfs = _project_sample(xs, wmain_s, wf_s, bfp, cos_s, sin_s)
        ps3 = ps.reshape(nseq, t_new, N_GROUPS * GROUP)
        ya_s = _moba_decode(kt_a, vt_a, page_table, l, ps3, qg=0, kg=1, vg=2)
        lfn = jnp.transpose(lfs[:, :n_heads].reshape(nseq, t_new, n_heads), (0, 2, 1))
        lfn = jnp.pad(lfn, ((0, 0), (0, 0), (0, LANES - t_new)))
        yb_s = _fox_decode(kt_b, vt_b, lft_b, page_table, l, ps3, lfn, qg=3, kg=4, vg=5)
        res_s = _merge(ya_s.reshape(n_s, GROUP), yb_s.reshape(n_s, GROUP), ps, 3, 4, xs, w_pa[l], w_pb[l], w_o[l],
                       g0, b0, wr, tm=n_s, hp=True, alpha=alpha, n_exp=n_exp if moe else 0)

        if not moe:
            zero_te = jnp.zeros((n_p // tm,), jnp.int32)
            wg, wu, wd = w_gate_d[l // 2][None], w_up_d[l // 2][None], w_down_d[l // 2][None]
            xp = _ffn(res_p[0], ones_p, zero_te, jnp.full((1,), n_p // tm, jnp.int32),
                      wg.astype(BF16), wu.astype(BF16), wd.astype(BF16), g1, b1,
                      tm=tm, tf=512, hp=False, alpha=alpha, fuse_ln=True)
            xs = _ffn(res_s[0], ones_s, zero_te[:1], jnp.ones((1,), jnp.int32), wg, wu, wd, g1, b1,
                      tm=n_s, tf=512, hp=True, alpha=alpha, fuse_ln=True)
        else:
            x1 = jnp.concatenate([res_p[0], res_s[0]], axis=0)
            top_idx = jnp.concatenate([res_p[1][:, :TOP_K], res_s[1][:, :TOP_K]], axis=0)
            top_w = jnp.concatenate([res_p[2][:, :TOP_K], res_s[2][:, :TOP_K]], axis=0)
            row_token, row_gate, dest, tile_expert, n_live = _moe_plan(top_idx, top_w, n_exp, tm)
            xs_rows = jnp.take(x1, row_token, axis=0, mode="clip")
            ys = _ffn(xs_rows, row_gate[:, None], tile_expert, n_live,
                      w_gate_e[l // 2].astype(BF16), w_up_e[l // 2].astype(BF16), w_down_e[l // 2].astype(BF16),
                      g1, b1, tm=tm, tf=512, hp=False, alpha=alpha, fuse_ln=False)
            pick = lambda rows, k: jnp.take(ys, rows[:, k], axis=0, mode="clip")
            xp = _combine(res_p[0], pick(dest[:n_p], 0), pick(dest[:n_p], 1), g1, b1, tm=tm, alpha=alpha)
            xs = _combine(res_s[0], pick(dest[n_p:], 0), pick(dest[n_p:], 1), g1, b1, tm=n_s, alpha=alpha)

        for j, grp in enumerate((1, 2, 4, 5)):
            outs_s[j].append(ps[:, grp * GROUP:(grp + 1) * GROUP].reshape(nseq, t_new, n_heads, HEAD_DIM))
        outs_s[4].append(lfs[:, :n_heads].reshape(nseq, t_new, n_heads))

    ka, va, kb, vb, lfp = state["bufs"]
    heads = lambda a: a.reshape(depth, bsz, seq, n_heads, HEAD_DIM)
    return (xp.reshape(bsz, seq, d), xs.reshape(nseq, t_new, d),
            heads(ka), heads(va), heads(kb), heads(vb), lfp[:, :, :n_heads].reshape(depth, bsz, seq, n_heads),
            *[jnp.stack(o) for o in outs_s])
```

```python
import functools

import jax
import jax.numpy as jnp
from jax import lax
from jax.experimental import pallas as pl
from jax.experimental.pallas import tpu as pltpu

HEAD_DIM = 64
MOBA_BLOCK = 256
MOBA_TOPK = 3
TOP_K = 2
ROPE_THETA = 10000.0
LN_EPS = 1e-5
NEG = -1e30
LANES = 128
GROUP = 512
N_GROUPS = 10
PAGE_ROWS = 128
VMEM_LIMIT = 56 * 1024 * 1024
HI = lax.Precision.HIGHEST
F32 = jnp.float32
BF16 = jnp.bfloat16


def _dot(a, b, hp):
    if hp:
        return jnp.dot(a, b, precision=HI, preferred_element_type=F32)
    return jnp.dot(a, b, preferred_element_type=F32)


def _dot_nt(a, b, hp):
    dn = (((1,), (1,)), ((), ()))
    if hp:
        return lax.dot_general(a, b, dn, precision=HI, preferred_element_type=F32)
    return lax.dot_general(a, b, dn, preferred_element_type=F32)


def _split_hi_lo(a):
    hi = a.astype(BF16)
    lo = (a - hi.astype(F32)).astype(BF16)
    return hi, lo


def _stack_hi_lo(a):
    hi, lo = _split_hi_lo(a)
    return jnp.concatenate([hi, lo], axis=0)


def _dot3(a2, b, *, nt):
    r = a2.shape[0] // 2
    b_hi, b_lo = _split_hi_lo(b)
    f = _dot_nt if nt else _dot
    s2 = f(a2, b_hi, False)
    return s2[:r] + s2[r:] + f(a2[:r], b_lo, False)


def _dot2(a2, b, *, nt):
    r = a2.shape[0] // 2
    s2 = (_dot_nt if nt else _dot)(a2, b, False)
    return s2[:r] + s2[r:]


def _params(sem):
    return pltpu.CompilerParams(dimension_semantics=sem, vmem_limit_bytes=VMEM_LIMIT)


def _layer_norm(z, g, b):
    mu = jnp.mean(z, axis=-1, keepdims=True)
    zc = z - mu
    var = jnp.mean(zc * zc, axis=-1, keepdims=True)
    return zc * lax.rsqrt(var + LN_EPS) * g + b


def _log_sigmoid(z):
    return jnp.minimum(z, 0.0) - jnp.log1p(jnp.exp(-jnp.abs(z)))


def _rope_cols(r, cos, sin):
    lane = lax.broadcasted_iota(jnp.int32, cos.shape, 1)
    first = (lane % HEAD_DIM) < (HEAD_DIM // 2)
    out = []
    for c in range(GROUP // LANES):
        seg = r[:, c * LANES:(c + 1) * LANES]
        rot = jnp.where(first, pltpu.roll(seg, LANES - HEAD_DIM // 2, 1), pltpu.roll(seg, HEAD_DIM // 2, 1))
        out.append(seg * cos + rot * sin)
    return jnp.concatenate(out, axis=1)


def _proj_prompt_kernel(x_ref, w_ref, wf_ref, bf_ref, cos_ref, sin_ref, *refs):
    q_ref, ka_ref, va_ref, kb_ref, vb_ref, kv_ref, g_ref, lf_ref = refs[-8:]
    scale = HEAD_DIM ** -0.5
    xb = x_ref[...].astype(BF16)
    cos, sin = cos_ref[...], sin_ref[...]

    def col(g):
        return _dot(xb, w_ref[:, g * GROUP:(g + 1) * GROUP], False)

    def emit(slot, t_ref, r):
        t_ref[...] = jnp.transpose(r)
        kv_ref[:, slot * GROUP:(slot + 1) * GROUP] = r.astype(BF16)

    lf_ref[...] = _log_sigmoid(_dot(xb, wf_ref[...], False) + bf_ref[...])
    q_ref[:, 0:GROUP] = (_rope_cols(col(0), cos, sin) * scale).astype(BF16)
    emit(0, ka_ref, _rope_cols(col(1), cos, sin))
    emit(1, va_ref, col(2))
    q_ref[:, GROUP:2 * GROUP] = (col(3) * scale).astype(BF16)
    emit(2, kb_ref, col(4))
    emit(3, vb_ref, col(5))
    for g in range(6, N_GROUPS):
        g_ref[:, (g - 6) * GROUP:(g - 5) * GROUP] = jax.nn.sigmoid(col(g)).astype(BF16)


def _project_prompt(x, w, wf, bfp, cos, sin, state, layer, *, tm, table_tiles):
    n, d = x.shape
    depth = state["depth"]
    bsz = n // (table_tiles * tm)
    row = lambda width: pl.BlockSpec((tm, width), lambda i: (i, 0))
    slab = lambda width: pl.BlockSpec((None, tm, width), lambda i: (layer, i, 0))
    slab_t = pl.BlockSpec((None, None, GROUP, tm), lambda i: (layer, i // table_tiles, 0, i % table_tiles))
    in_specs = [
        row(d),
        pl.BlockSpec((d, N_GROUPS * GROUP), lambda i: (0, 0), pipeline_mode=pl.Buffered(1)),
        pl.BlockSpec((d, LANES), lambda i: (0, 0)),
        pl.BlockSpec((1, LANES), lambda i: (0, 0)),
        pl.BlockSpec((tm, LANES), lambda i: (i % table_tiles, 0)),
        pl.BlockSpec((tm, LANES), lambda i: (i % table_tiles, 0)),
    ]
    args = [x, w, wf, bfp, cos, sin]
    aliases = {}
    if state["bufs"] is not None:
        in_specs += [pl.BlockSpec(memory_space=pl.ANY)] * 5
        args += list(state["bufs"])
        aliases = {6 + k: 1 + k for k in range(4)}
        aliases[10] = 7
    out = pl.pallas_call(
        _proj_prompt_kernel,
        grid=(n // tm,),
        in_specs=in_specs,
        out_specs=[row(2 * GROUP), slab_t, slab_t, slab_t, slab_t, row(4 * GROUP), row(4 * GROUP), slab(LANES)],
        out_shape=[jax.ShapeDtypeStruct((n, 2 * GROUP), BF16)]
        + [jax.ShapeDtypeStruct((depth, bsz, GROUP, table_tiles * tm), F32)] * 4
        + [jax.ShapeDtypeStruct((n, 4 * GROUP), BF16), jax.ShapeDtypeStruct((n, 4 * GROUP), BF16),
           jax.ShapeDtypeStruct((depth, n, LANES), F32)],
        input_output_aliases=aliases,
        compiler_params=_params(("parallel",)),
        name="project_prompt",
    )(*args)
    q, ka, va, kb, vb, kv, gates, lf = out
    state["bufs"] = (ka, va, kb, vb, lf)
    return q, kv, gates


def _proj_kernel(x_ref, w_ref, wf_ref, bf_ref, cos_ref, sin_ref, p_ref, lf_ref):
    j = pl.program_id(1)
    scale = HEAD_DIM ** -0.5
    x = x_ref[...].astype(BF16)

    @pl.when(j == 0)
    def _():
        lf_ref[...] = _log_sigmoid(_dot(x, wf_ref[...], False) + bf_ref[...])

    r = _dot(x, w_ref[...], False)

    @pl.when(j == 0)
    def _():
        p_ref[...] = _rope_cols(r, cos_ref[...], sin_ref[...]) * scale

    @pl.when(j == 1)
    def _():
        p_ref[...] = _rope_cols(r, cos_ref[...], sin_ref[...])

    @pl.when(j == 3)
    def _():
        p_ref[...] = r * scale

    @pl.when((j == 2) | (j == 4) | (j == 5))
    def _():
        p_ref[...] = r

    @pl.when(j >= 6)
    def _():
        p_ref[...] = jax.nn.sigmoid(r)


def _project_sample(x, w, wf, bfp, cos, sin):
    n, d = x.shape
    return pl.pallas_call(
        _proj_kernel,
        grid=(1, N_GROUPS),
        in_specs=[
            pl.BlockSpec((n, d), lambda i, j: (0, 0)),
            pl.BlockSpec((d, GROUP), lambda i, j: (0, j)),
            pl.BlockSpec((d, LANES), lambda i, j: (0, 0)),
            pl.BlockSpec((1, LANES), lambda i, j: (0, 0)),
            pl.BlockSpec((n, LANES), lambda i, j: (0, 0)),
            pl.BlockSpec((n, LANES), lambda i, j: (0, 0)),
        ],
        out_specs=[
            pl.BlockSpec((n, GROUP), lambda i, j: (0, j)),
            pl.BlockSpec((n, LANES), lambda i, j: (0, 0)),
        ],
        out_shape=[jax.ShapeDtypeStruct((n, N_GROUPS * GROUP), F32),
                   jax.ShapeDtypeStruct((n, LANES), F32)],
        compiler_params=_params(("parallel", "arbitrary")),
        name="project_sample",
    )(x, w, wf, bfp, cos, sin)


def _cumsum_kernel(lf_ref, c_ref):
    ch = 256
    s_len = lf_ref.shape[0]
    row = lax.broadcasted_iota(jnp.int32, (ch, ch), 0)
    col = lax.broadcasted_iota(jnp.int32, (ch, ch), 1)
    tri = (col <= row).astype(F32)

    def body(k, carry):
        rows = pl.ds(pl.multiple_of(k * ch, ch), ch)
        c = _dot(tri, lf_ref[rows, :], True) + carry
        c_ref[rows, :] = c
        return c[ch - 1:ch, :]

    lax.fori_loop(0, s_len // ch, body, jnp.zeros((1, LANES), F32))


def _cumsum(lf4, layer):
    _, b, s, _ = lf4.shape
    return pl.pallas_call(
        _cumsum_kernel,
        grid=(b,),
        in_specs=[pl.BlockSpec((None, None, s, LANES), lambda i: (layer, i, 0, 0))],
        out_specs=pl.BlockSpec((None, s, LANES), lambda i: (i, 0, 0)),
        out_shape=jax.ShapeDtypeStruct((b, s, LANES), F32),
        compiler_params=_params(("parallel",)),
        name="fox_cumsum",
    )(lf4)


def _split3(c):
    hi = c.astype(BF16).astype(F32)
    mid = (c - hi).astype(BF16).astype(F32)
    lo = c - hi - mid
    return hi, mid, lo


def _place(lane, base, vals):
    out = jnp.zeros(lane.shape, F32)
    for k, v in enumerate(vals):
        out = jnp.where(lane == base + k, v, out)
    return out


def _attn_prompt_kernel(*refs, mode, tq, tk, n_blk):
    if mode == "fox":
        q_ref, k_ref, v_ref, c_ref, y_ref, kx_ref, vx_ref, qx_ref, m_ref, acc_ref = refs
    else:
        q_ref, k_ref, v_ref, y_ref, kx_ref, vx_ref, qx_ref, m_ref, acc_ref, mean_ref = refs
    hp = pl.program_id(1)
    i = pl.program_id(2)
    s_len = k_ref.shape[0]
    pc = 512
    bases = (HEAD_DIM, 0)

    @pl.when(i == 0)
    def _():
        def body(t, _):
            rows = pl.ds(pl.multiple_of(t * pc, pc), pc)
            kc = k_ref[rows, :].astype(F32)
            vc = v_ref[rows, :].astype(F32)
            lane = lax.broadcasted_iota(jnp.int32, (pc, LANES), 1)
            if mode == "moba":
                for u in range(pc // MOBA_BLOCK):
                    mean = jnp.sum(kc[u * MOBA_BLOCK:(u + 1) * MOBA_BLOCK, :], axis=0, keepdims=True)
                    mean_ref[pl.ds(t * (pc // MOBA_BLOCK) + u, 1), :] = mean * (1.0 / MOBA_BLOCK)
            for head in range(2):
                base = bases[head]
                own = lane // HEAD_DIM == head
                if mode == "fox":
                    h = 2 * hp + head
                    ck = jnp.sum(jnp.where(lane == h, c_ref[rows, :], 0.0), axis=1, keepdims=True)
                    hi, mid, lo = _split3(ck)
                    ex = _place(lane, base, [1.0, 1.0, 1.0, -hi, -mid, -lo])
                else:
                    blk = (t * pc + lax.broadcasted_iota(jnp.int32, (pc, LANES), 0)) // MOBA_BLOCK
                    ex = jnp.where(lane - base == blk, 1.0, 0.0)
                kx_ref[head, rows, :] = jnp.where(own, kc, ex).astype(BF16)
                vx_ref[head, rows, :] = jnp.where(own, vc, 1.0).astype(BF16)
            return 0

        lax.fori_loop(0, s_len // pc, body, 0)

    qq = q_ref[...].astype(F32)
    lane = lax.broadcasted_iota(jnp.int32, (tq, LANES), 1)
    for head in range(2):
        base = bases[head]
        own = lane // HEAD_DIM == head
        if mode == "fox":
            h = 2 * hp + head
            cq = jnp.sum(jnp.where(lane == h, c_ref[pl.ds(pl.multiple_of(i * tq, tq), tq), :], 0.0),
                         axis=1, keepdims=True)
            hi, mid, lo = _split3(cq)
            ex = _place(lane, base, [hi, mid, lo, 1.0, 1.0, 1.0])
        else:
            gate_t = _dot_nt(mean_ref[...].astype(BF16), jnp.where(own, qq, 0.0).astype(BF16), False)
            nidx = lax.broadcasted_iota(jnp.int32, (n_blk, tq), 0)
            cb = (i * tq + lax.broadcasted_iota(jnp.int32, (1, tq), 1)) // MOBA_BLOCK
            rank = jnp.zeros((n_blk, tq), F32)
            for n2 in range(n_blk):
                g2 = gate_t[n2:n2 + 1, :]
                beats = (g2 > gate_t) | ((g2 == gate_t) & (n2 < nidx))
                rank = rank + jnp.where(beats & (n2 < cb), 1.0, 0.0)
            keep = ((nidx < cb) & (rank < MOBA_TOPK)) | (nidx == cb)
            pen_t = jnp.where(keep, 0.0, NEG)
            pieces = [pen_t, jnp.full((LANES - n_blk, tq), NEG, F32)]
            if base:
                pieces = [jnp.full((base, tq), NEG, F32), pen_t, jnp.full((LANES - base - n_blk, tq), NEG, F32)]
            ex = jnp.transpose(jnp.concatenate(pieces, axis=0))
        qx_ref[head] = jnp.where(own, qq, ex).astype(BF16)
    m_ref[...] = jnp.full(m_ref.shape, NEG, F32)
    acc_ref[...] = jnp.zeros(acc_ref.shape, F32)

    def chunk(j, masked):
        rows = pl.ds(pl.multiple_of(j * tk, tk), tk)
        for head in range(2):
            s = _dot_nt(qx_ref[head], kx_ref[head, rows, :], False)
            if masked:
                kpos = j * tk + lax.broadcasted_iota(jnp.int32, (tq, tk), 1)
                qp = i * tq + lax.broadcasted_iota(jnp.int32, (tq, tk), 0)
                s = jnp.where(kpos <= qp, s, NEG)
            m_prev = m_ref[head]
            m_new = jnp.maximum(m_prev, jnp.max(s, axis=1, keepdims=True))
            p = jnp.exp(s - jnp.tile(m_new, (1, tk // LANES)))
            acc_ref[head] = (jnp.exp(m_prev - m_new) * acc_ref[head]
                             + _dot(p.astype(BF16), vx_ref[head, rows, :], False))
            m_ref[head] = m_new

    per_tile = tq // tk
    n_full = i * per_tile

    def body(t, _):
        for d in range(per_tile):
            chunk(t * per_tile + d, False)
        return 0

    lax.fori_loop(0, i, body, 0)
    for d in range(per_tile):
        chunk(n_full + d, True)

    first = lane < HEAD_DIM
    a0, a1 = acc_ref[0], acc_ref[1]
    num = jnp.where(first, a0, a1)
    den = jnp.where(first, pltpu.roll(a0, HEAD_DIM, 1), pltpu.roll(a1, HEAD_DIM, 1))
    y_ref[...] = (num / den).astype(y_ref.dtype)


def _attn_prompt(q3, kv3, c3, *, mode, qoff, koff, tq=512, tk=256):
    b, s, _ = q3.shape
    pairs = GROUP // LANES
    n_blk = s // MOBA_BLOCK
    assert s % tq == 0 and tq % tk == 0 and tk == MOBA_BLOCK and n_blk <= HEAD_DIM and n_blk % 8 == 0
    kv_spec = lambda g: pl.BlockSpec((None, s, LANES), lambda bi, h, i: (bi, 0, g * pairs + h))
    in_specs = [pl.BlockSpec((None, tq, LANES), lambda bi, h, i: (bi, i, qoff * pairs + h)),
                kv_spec(koff), kv_spec(koff + 1)]
    args = [q3, kv3, kv3]
    scratch = [pltpu.VMEM((2, s, LANES), BF16), pltpu.VMEM((2, s, LANES), BF16), pltpu.VMEM((2, tq, LANES), BF16),
               pltpu.VMEM((2, tq, LANES), F32), pltpu.VMEM((2, tq, LANES), F32)]
    if mode == "fox":
        in_specs.append(pl.BlockSpec((None, s, LANES), lambda bi, h, i: (bi, 0, 0)))
        args.append(c3)
    else:
        scratch.append(pltpu.VMEM((n_blk, LANES), F32))
    return pl.pallas_call(
        functools.partial(_attn_prompt_kernel, mode=mode, tq=tq, tk=tk, n_blk=n_blk),
        grid=(b, pairs, s // tq),
        in_specs=in_specs,
        out_specs=pl.BlockSpec((None, tq, LANES), lambda bi, h, i: (bi, i, h)),
        out_shape=jax.ShapeDtypeStruct((b, s, GROUP), BF16),
        scratch_shapes=scratch,
        compiler_params=_params(("parallel", "parallel", "arbitrary")),
        name=mode + "_prompt",
    )(*args)


def _merge_kernel(*refs, hp, alpha, n_exp):
    if n_exp:
        (ya_ref, yb_ref, ga_ref, gb_ref, x_ref, wpa_ref, wpb_ref, wo_ref, g_ref, b_ref, wr_ref,
         o_ref, idx_ref, wt_ref) = refs
    else:
        ya_ref, yb_ref, ga_ref, gb_ref, x_ref, wpa_ref, wpb_ref, wo_ref, g_ref, b_ref, o_ref = refs
    cdt = F32 if hp else BF16
    m = (ga_ref[...].astype(F32) * _dot(ya_ref[...].astype(cdt), wpa_ref[...], hp)
         + gb_ref[...].astype(F32) * _dot(yb_ref[...].astype(cdt), wpb_ref[...], hp))
    h = _dot(m.astype(cdt), wo_ref[...], hp)
    x1 = _layer_norm(alpha * x_ref[...] + h, g_ref[...], b_ref[...])
    o_ref[...] = x1
    if n_exp:
        logits = _dot(x1.astype(BF16), wr_ref[...].astype(BF16), False)
        lane = lax.broadcasted_iota(jnp.int32, logits.shape, 1)
        logits = jnp.where(lane < n_exp, logits, NEG)
        v1 = jnp.max(logits, axis=1, keepdims=True)
        i1 = jnp.min(jnp.where(logits == v1, lane, LANES), axis=1, keepdims=True)
        rest = jnp.where(lane == i1, NEG, logits)
        v2 = jnp.max(rest, axis=1, keepdims=True)
        i2 = jnp.min(jnp.where(rest == v2, lane, LANES), axis=1, keepdims=True)
        e = jnp.exp(v2 - v1)
        w1 = 1.0 / (1.0 + e)
        w2 = e / (1.0 + e)
        idx_ref[...] = jnp.where(lane == 0, i1, jnp.where(lane == 1, i2, 0))
        wt_ref[...] = jnp.where(lane == 0, w1, jnp.where(lane == 1, w2, 0.0))


def _merge(ya, yb, gates, ga_blk, gb_blk, x, wpa, wpb, wo, g, b, wr, *, tm, hp, alpha, n_exp):
    n, d = x.shape
    assert d == 2 * GROUP
    in_specs = [
        pl.BlockSpec((tm, GROUP), lambda i: (i, 0)),
        pl.BlockSpec((tm, GROUP), lambda i: (i, 0)),
        pl.BlockSpec((tm, d), lambda i: (i, ga_blk)),
        pl.BlockSpec((tm, d), lambda i: (i, gb_blk)),
        pl.BlockSpec((tm, d), lambda i: (i, 0)),
        pl.BlockSpec((GROUP, d), lambda i: (0, 0)),
        pl.BlockSpec((GROUP, d), lambda i: (0, 0)),
        pl.BlockSpec((d, d), lambda i: (0, 0)),
        pl.BlockSpec((1, d), lambda i: (0, 0)),
        pl.BlockSpec((1, d), lambda i: (0, 0)),
    ]
    args = [ya, yb, gates, gates, x, wpa, wpb, wo, g, b]
    out_specs = [pl.BlockSpec((tm, d), lambda i: (i, 0))]
    out_shape = [jax.ShapeDtypeStruct((n, d), F32)]
    if n_exp:
        in_specs.append(pl.BlockSpec((d, LANES), lambda i: (0, 0)))
        args.append(wr)
        out_specs += [pl.BlockSpec((tm, LANES), lambda i: (i, 0))] * 2
        out_shape += [jax.ShapeDtypeStruct((n, LANES), jnp.int32), jax.ShapeDtypeStruct((n, LANES), F32)]
    return pl.pallas_call(
        functools.partial(_merge_kernel, hp=hp, alpha=alpha, n_exp=n_exp),
        grid=(n // tm,),
        in_specs=in_specs,
        out_specs=out_specs,
        out_shape=out_shape,
        compiler_params=_params(("parallel",)),
        name="merge_out",
    )(*args)


def _ffn_kernel(te_ref, nv_ref, x_ref, sc_ref, wg_ref, wu_ref, wd_ref, g_ref, b_ref, o_ref, xb_ref, acc_ref,
                *, hp, alpha, fuse_ln):
    i = pl.program_id(0)
    f = pl.program_id(1)
    cdt = F32 if hp else BF16
    live = i < nv_ref[0]

    @pl.when(f == 0)
    def _():
        xb_ref[...] = x_ref[...].astype(cdt)
        acc_ref[...] = jnp.zeros(acc_ref.shape, F32)

    @pl.when(live)
    def _():
        xb = xb_ref[...]
        a = _dot(xb, wg_ref[...], hp)
        u = _dot(xb, wu_ref[...], hp)
        hmid = a * jax.nn.sigmoid(a) * u
        acc_ref[...] += _dot(hmid.astype(cdt), wd_ref[...], hp)

    @pl.when(f == pl.num_programs(1) - 1)
    def _():
        if fuse_ln:
            o_ref[...] = _layer_norm(alpha * x_ref[...] + acc_ref[...], g_ref[...], b_ref[...])
        else:
            o_ref[...] = acc_ref[...] * sc_ref[...]


def _ffn(x, scale, tile_expert, n_live, wg, wu, wd, g, b, *, tm, tf, hp, alpha, fuse_ln):
    r, d = x.shape
    _, _, dff = wg.shape
    grid_spec = pltpu.PrefetchScalarGridSpec(
        num_scalar_prefetch=2,
        grid=(r // tm, dff // tf),
        in_specs=[
            pl.BlockSpec((tm, d), lambda i, f, te, nv: (i, 0)),
            pl.BlockSpec((tm, 1), lambda i, f, te, nv: (i, 0)),
            pl.BlockSpec((None, d, tf), lambda i, f, te, nv: (te[i], 0, f)),
            pl.BlockSpec((None, d, tf), lambda i, f, te, nv: (te[i], 0, f)),
            pl.BlockSpec((None, tf, d), lambda i, f, te, nv: (te[i], f, 0)),
            pl.BlockSpec((1, d), lambda i, f, te, nv: (0, 0)),
            pl.BlockSpec((1, d), lambda i, f, te, nv: (0, 0)),
        ],
        out_specs=pl.BlockSpec((tm, d), lambda i, f, te, nv: (i, 0)),
        scratch_shapes=[pltpu.VMEM((tm, d), F32 if hp else BF16), pltpu.VMEM((tm, d), F32)],
    )
    return pl.pallas_call(
        functools.partial(_ffn_kernel, hp=hp, alpha=alpha, fuse_ln=fuse_ln),
        grid_spec=grid_spec,
        out_shape=jax.ShapeDtypeStruct((r, d), F32),
        compiler_params=_params(("parallel", "arbitrary")),
        name="swiglu",
    )(tile_expert, n_live, x, scale, wg, wu, wd, g, b)


def _combine_kernel(x_ref, y1_ref, y2_ref, g_ref, b_ref, o_ref, *, alpha):
    o_ref[...] = _layer_norm(alpha * x_ref[...] + (y1_ref[...] + y2_ref[...]), g_ref[...], b_ref[...])


def _combine(x, y1, y2, g, b, *, tm, alpha):
    n, d = x.shape
    row = pl.BlockSpec((tm, d), lambda i: (i, 0))
    vec = pl.BlockSpec((1, d), lambda i: (0, 0))
    return pl.pallas_call(
        functools.partial(_combine_kernel, alpha=alpha),
        grid=(n // tm,),
        in_specs=[row, row, row, vec, vec],
        out_specs=row,
        out_shape=jax.ShapeDtypeStruct((n, d), F32),
        compiler_params=_params(("parallel",)),
        name="moe_combine",
    )(x, y1, y2, g, b)


def _page_specs(layer, g_pages, rows, chunk_of):
    def spec(g):
        return pl.BlockSpec((None, None, rows, PAGE_ROWS),
                            lambda n, c, pt: (layer, pt[n, chunk_of(c) * g_pages + g], 0, 0))
    return [spec(g) for g in range(g_pages)]


def _build_qbd(q, t_new):
    n_heads = GROUP // HEAD_DIM
    lane_head = lax.broadcasted_iota(jnp.int32, (n_heads, GROUP), 1) // HEAD_DIM
    row_head = lax.broadcasted_iota(jnp.int32, (n_heads, GROUP), 0)
    parts = [jnp.where(lane_head == row_head, jnp.broadcast_to(q[qi:qi + 1, :], (n_heads, GROUP)), 0.0)
             for qi in range(t_new)]
    return jnp.concatenate(parts, axis=0)


def _fold_heads(o, t_new):
    n_heads = GROUP // HEAD_DIM
    rows = t_new * n_heads
    lane_head = lax.broadcasted_iota(jnp.int32, (rows, GROUP), 1) // HEAD_DIM
    row_head = lax.broadcasted_iota(jnp.int32, (rows, GROUP), 0) % n_heads
    om = jnp.where(lane_head == row_head, o, 0.0)
    return [jnp.sum(om[qi * n_heads:(qi + 1) * n_heads, :], axis=0, keepdims=True) for qi in range(t_new)]


def _new_key_tiles(pad_ref, kn_ref, vn_ref, t_new):
    pad_ref[...] = jnp.zeros(pad_ref.shape, F32)
    pad_ref[0, 0:t_new, :] = kn_ref[...]
    pad_ref[1, 0:t_new, :] = vn_ref[...]


def _fox_decode_kernel(pt_ref, *refs, g_pages, t_new):
    kt_refs = refs[:g_pages]
    vt_refs = refs[g_pages:2 * g_pages]
    lf_refs = refs[2 * g_pages:3 * g_pages]
    q_ref, lfn_ref, kn_ref, vn_ref, y_ref, q2_ref, m_ref, l_ref, acc_ref, car_ref, pad_ref = refs[3 * g_pages:]
    c = pl.program_id(1)
    n_heads = GROUP // HEAD_DIM
    rows = t_new * n_heads

    @pl.when(c == 0)
    def _():
        q2_ref[0:rows, :] = _build_qbd(q_ref[...], t_new).astype(BF16)
        m_ref[...] = jnp.full(m_ref.shape, NEG, F32)
        l_ref[...] = jnp.zeros(l_ref.shape, F32)
        acc_ref[...] = jnp.zeros(acc_ref.shape, F32)
        car_ref[...] = jnp.zeros(car_ref.shape, F32)

    lane = lax.broadcasted_iota(jnp.int32, (LANES, LANES), 1)
    srow = lax.broadcasted_iota(jnp.int32, (LANES, LANES), 0)
    cn = _dot(lfn_ref[...], (srow <= lane).astype(F32), True)

    def update(s, pv):
        m_prev = m_ref[...]
        m_new = jnp.maximum(m_prev, jnp.max(s, axis=1, keepdims=True))
        p = jnp.exp(s - jnp.tile(m_new, (1, s.shape[1] // LANES)))
        alpha = jnp.exp(m_prev - m_new)
        l_ref[...] = alpha * l_ref[...] + jnp.sum(p, axis=1, keepdims=True)
        acc_ref[...] = jnp.tile(alpha, (1, GROUP // LANES)) * acc_ref[...] + pv(p)
        m_ref[...] = m_new

    hl = lax.broadcasted_iota(jnp.int32, (n_heads, LANES), 1)
    car = car_ref[...]
    d_pages = [None] * g_pages
    for g in reversed(range(g_pages)):
        x = lf_refs[g][...]
        y = x
        k = 1
        while k < LANES:
            y = y + jnp.where(hl < LANES - k, pltpu.roll(y, LANES - k, 1), 0.0)
            k *= 2
        d_pages[g] = (y - x) + car
        car = car + jnp.broadcast_to(y[:, 0:1], (n_heads, LANES))
    car_ref[...] = car
    dsuf = jnp.concatenate(d_pages, axis=1)
    bias = jnp.concatenate([dsuf + cn[:, qi:qi + 1] for qi in range(t_new)], axis=0)
    q1 = q2_ref[0:rows, :]
    kt = jnp.concatenate([r[...] for r in kt_refs], axis=1).astype(BF16)
    vt = jnp.concatenate([r[...] for r in vt_refs], axis=1).astype(BF16)
    update(_dot(q1, kt, False) + bias, lambda p: _dot_nt(p.astype(BF16), vt, False))

    @pl.when(c == pl.num_programs(1) - 1)
    def _():
        _new_key_tiles(pad_ref, kn_ref, vn_ref, t_new)
        s_new = _dot_nt(q1, pad_ref[0].astype(BF16), False)
        b_new = [jnp.where(hl <= qi, cn[:, qi:qi + 1] - cn, NEG) for qi in range(t_new)]
        update(s_new + jnp.concatenate(b_new, axis=0),
               lambda p: _dot(p.astype(BF16), pad_ref[1].astype(BF16), False))
        o = acc_ref[...] / jnp.tile(l_ref[...], (1, GROUP // LANES))
        for qi, r in enumerate(_fold_heads(o, t_new)):
            y_ref[qi:qi + 1, :] = r


def _fox_decode(kt, vt, lft, page_table, layer, p3, lfn, *, qg, kg, vg, g_pages=8):
    n, n_pages = page_table.shape
    t_new = p3.shape[1]
    n_heads = GROUP // HEAD_DIM
    rows = t_new * n_heads
    nc = n_pages // g_pages
    rev = lambda c: nc - 1 - c
    in_specs = (_page_specs(layer, g_pages, GROUP, rev) + _page_specs(layer, g_pages, GROUP, rev)
                + _page_specs(layer, g_pages, n_heads, rev) + [
        pl.BlockSpec((None, t_new, GROUP), lambda i, c, pt: (i, 0, qg)),
        pl.BlockSpec((None, n_heads, LANES), lambda i, c, pt: (i, 0, 0)),
        pl.BlockSpec((None, t_new, GROUP), lambda i, c, pt: (i, 0, kg)),
        pl.BlockSpec((None, t_new, GROUP), lambda i, c, pt: (i, 0, vg)),
    ])
    grid_spec = pltpu.PrefetchScalarGridSpec(
        num_scalar_prefetch=1,
        grid=(n, nc),
        in_specs=in_specs,
        out_specs=pl.BlockSpec((None, t_new, GROUP), lambda i, c, pt: (i, 0, 0)),
        scratch_shapes=[pltpu.VMEM((2 * rows, GROUP), BF16), pltpu.VMEM((rows, LANES), F32),
                        pltpu.VMEM((rows, LANES), F32), pltpu.VMEM((rows, GROUP), F32),
                        pltpu.VMEM((n_heads, LANES), F32), pltpu.VMEM((2, LANES, GROUP), F32)],
    )
    return pl.pallas_call(
        functools.partial(_fox_decode_kernel, g_pages=g_pages, t_new=t_new),
        grid_spec=grid_spec,
        out_shape=jax.ShapeDtypeStruct((n, t_new, GROUP), F32),
        compiler_params=_params(("parallel", "arbitrary")),
        name="fox_sample",
    )(page_table, *([kt] * g_pages), *([vt] * g_pages), *([lft] * g_pages), p3, lfn, p3, p3)


def _moba_decode_kernel(pt_ref, *refs, g_pages, t_new, n_blk):
    kt_refs = refs[:g_pages]
    vt_refs = refs[g_pages:2 * g_pages]
    (q_ref, kn_ref, vn_ref, y_ref, q2_ref, s_ref, gate_ref, bmax_ref, sel_ref, m_ref, l_ref, acc_ref,
     pad_ref) = refs[2 * g_pages:]
    c = pl.program_id(1)
    nc = pl.num_programs(1) // 2
    n_heads = GROUP // HEAD_DIM
    rows = t_new * n_heads
    ck = g_pages * PAGE_ROWS
    bpc = ck // MOBA_BLOCK
    lane = lax.broadcasted_iota(jnp.int32, (rows, LANES), 1)

    @pl.when(c == 0)
    def _():
        q2_ref[0:rows, :] = _build_qbd(q_ref[...], t_new).astype(BF16)
        gate_ref[...] = jnp.zeros(gate_ref.shape, F32)
        bmax_ref[...] = jnp.full(bmax_ref.shape, NEG, F32)

    @pl.when(c < nc)
    def _():
        kt32 = jnp.concatenate([r[...] for r in kt_refs], axis=1)
        s = _dot(q2_ref[0:rows, :], kt32.astype(BF16), False)
        s_ref[:, pl.ds(pl.multiple_of(c * ck, ck), ck)] = s
        mlane = lax.broadcasted_iota(jnp.int32, (GROUP, LANES), 1)
        means, bmax = gate_ref[...], bmax_ref[...]
        for b in range(bpc):
            cols = slice(b * MOBA_BLOCK, (b + 1) * MOBA_BLOCK)
            mean = jnp.sum(kt32[:, cols], axis=1, keepdims=True) * (1.0 / MOBA_BLOCK)
            means = jnp.where(mlane == c * bpc + b, mean, means)
            bmax = jnp.where(lane == c * bpc + b, jnp.max(s[:, cols], axis=1, keepdims=True), bmax)
        gate_ref[...] = means
        bmax_ref[...] = bmax

    @pl.when(c == nc - 1)
    def _():
        gate = _dot(q2_ref[0:rows, :], gate_ref[...].astype(BF16), False)
        rank = jnp.zeros((rows, LANES), F32)
        for n2 in range(n_blk):
            g2 = gate[:, n2:n2 + 1]
            beats = (g2 > gate) | ((g2 == gate) & (n2 < lane))
            rank = rank + jnp.where(beats, 1.0, 0.0)
        sel = (lane < n_blk) & (rank < MOBA_TOPK)
        sel_ref[...] = jnp.where(sel, 1.0, 0.0)
        m_past = jnp.max(jnp.where(sel, bmax_ref[...], NEG), axis=1, keepdims=True)
        _new_key_tiles(pad_ref, kn_ref, vn_ref, t_new)
        hl = lax.broadcasted_iota(jnp.int32, (n_heads, LANES), 1)
        causal = jnp.concatenate([jnp.where(hl <= qi, 0.0, NEG) for qi in range(t_new)], axis=0)
        s_new = _dot_nt(q2_ref[0:rows, :], pad_ref[0].astype(BF16), False) + causal
        m = jnp.maximum(m_past, jnp.max(s_new, axis=1, keepdims=True)) + jnp.zeros((rows, LANES), F32)
        p_new = jnp.exp(s_new - m)
        m_ref[...] = m
        l_ref[...] = jnp.sum(p_new, axis=1, keepdims=True) + jnp.zeros((rows, LANES), F32)
        acc_ref[...] = _dot(p_new.astype(BF16), pad_ref[1].astype(BF16), False)

    @pl.when(c >= nc)
    def _():
        cc = c - nc
        s = s_ref[:, pl.ds(pl.multiple_of(cc * ck, ck), ck)]
        sel = sel_ref[...]
        m2 = jnp.tile(m_ref[...], (1, MOBA_BLOCK // LANES))
        ps = []
        for b in range(bpc):
            hit = jnp.max(jnp.where(lane == cc * bpc + b, sel, 0.0), axis=1, keepdims=True)
            sm = jnp.where(hit > 0.5, s[:, b * MOBA_BLOCK:(b + 1) * MOBA_BLOCK], NEG)
            ps.append(jnp.exp(sm - m2))
        p = jnp.concatenate(ps, axis=1)
        l_ref[...] += jnp.sum(p, axis=1, keepdims=True)
        vt = jnp.concatenate([r[...] for r in vt_refs], axis=1).astype(BF16)
        acc_ref[...] += _dot_nt(p.astype(BF16), vt, False)

    @pl.when(c == 2 * nc - 1)
    def _():
        o = acc_ref[...] / jnp.tile(l_ref[...], (1, GROUP // LANES))
        for qi, r in enumerate(_fold_heads(o, t_new)):
            y_ref[qi:qi + 1, :] = r


def _moba_decode(kt, vt, page_table, layer, p3, *, qg, kg, vg, g_pages=8):
    n, n_pages = page_table.shape
    t_new = p3.shape[1]
    n_heads = GROUP // HEAD_DIM
    rows = t_new * n_heads
    nc = n_pages // g_pages
    past = n_pages * PAGE_ROWS
    n_blk = past // MOBA_BLOCK
    assert MOBA_TOPK <= n_blk <= LANES
    in_specs = (_page_specs(layer, g_pages, GROUP, lambda c: jnp.minimum(c, nc - 1))
                + _page_specs(layer, g_pages, GROUP, lambda c: jnp.maximum(c - nc, 0)) + [
        pl.BlockSpec((None, t_new, GROUP), lambda i, c, pt: (i, 0, qg)),
        pl.BlockSpec((None, t_new, GROUP), lambda i, c, pt: (i, 0, kg)),
        pl.BlockSpec((None, t_new, GROUP), lambda i, c, pt: (i, 0, vg)),
    ])
    grid_spec = pltpu.PrefetchScalarGridSpec(
        num_scalar_prefetch=1,
        grid=(n, 2 * nc),
        in_specs=in_specs,
        out_specs=pl.BlockSpec((None, t_new, GROUP), lambda i, c, pt: (i, 0, 0)),
        scratch_shapes=[pltpu.VMEM((2 * rows, GROUP), BF16), pltpu.VMEM((rows, past), F32),
                        pltpu.VMEM((GROUP, LANES), F32)]
        + [pltpu.VMEM((rows, LANES), F32)] * 4
        + [pltpu.VMEM((rows, GROUP), F32), pltpu.VMEM((2, LANES, GROUP), F32)],
    )
    return pl.pallas_call(
        functools.partial(_moba_decode_kernel, g_pages=g_pages, t_new=t_new, n_blk=n_blk),
        grid_spec=grid_spec,
        out_shape=jax.ShapeDtypeStruct((n, t_new, GROUP), F32),
        compiler_params=_params(("parallel", "arbitrary")),
        name="moba_sample",
    )(page_table, *([kt] * g_pages), *([vt] * g_pages), p3, p3, p3)


def _rope_tables(pos):
    half = HEAD_DIM // 2
    inv = ROPE_THETA ** (-jnp.arange(half, dtype=F32) / half)
    ang = pos.astype(F32)[:, None] * inv[None, :]
    cos = jnp.cos(ang)
    sin = jnp.sin(ang)
    cos = jnp.concatenate([cos, cos, cos, cos], axis=1)
    sin = jnp.concatenate([-sin, sin, -sin, sin], axis=1)
    return cos, sin


def _split_w_in(w, dt):
    n_heads = GROUP // HEAD_DIM
    qkv = w[:, :6 * GROUP]
    wf = w[:, 6 * GROUP:6 * GROUP + n_heads]
    gates = w[:, 6 * GROUP + n_heads:]
    wmain = jnp.concatenate([qkv, gates], axis=1).astype(dt)
    wf = jnp.pad(wf, ((0, 0), (0, LANES - n_heads))).astype(dt)
    return wmain, wf


def _moe_plan(top_idx, top_w, n_exp, tm):
    n = top_idx.shape[0]
    n_pairs = TOP_K * n
    flat_e = top_idx.reshape(-1)
    onehot = (flat_e[None, :] == jnp.arange(n_exp, dtype=jnp.int32)[:, None]).astype(jnp.int32)
    seen = jnp.cumsum(onehot, axis=1)
    counts = seen[:, -1]
    padded = ((counts + tm - 1) // tm) * tm
    ends = jnp.cumsum(padded)
    starts = ends - padded
    dest = jnp.sum(onehot * (seen - 1 + starts[:, None]), axis=0)
    n_tiles = (n_pairs + tm - 1) // tm + n_exp
    tile_start = jnp.arange(n_tiles, dtype=jnp.int32) * tm
    tile_expert = jnp.minimum(jnp.sum((tile_start[:, None] >= ends[None, :]).astype(jnp.int32), axis=1), n_exp - 1)
    row_pair = jnp.full((n_tiles * tm,), -1, jnp.int32).at[dest].set(jnp.arange(n_pairs, dtype=jnp.int32))
    row_live = row_pair >= 0
    row_token = jnp.where(row_live, row_pair // TOP_K, 0)
    row_gate = jnp.where(row_live, jnp.take(top_w.reshape(-1), row_pair, mode="clip"), 0.0)
    n_live = (ends[-1] // tm).astype(jnp.int32).reshape(1)
    return row_token, row_gate, dest.reshape(n, TOP_K), tile_expert.astype(jnp.int32), n_live


def kernel(x_prompt, x_sample, cache_k_a, cache_v_a, cache_k_b, cache_v_b, cache_logf_b, page_table, w_in, b_f, w_pa,
           w_pb, w_o, ln_g, ln_b, w_gate_d, w_up_d, w_down_d, w_router, w_gate_e, w_up_e, w_down_e):
    bsz, seq, d = x_prompt.shape
    nseq, t_new, _ = x_sample.shape
    depth = w_in.shape[0]
    n_pool, page = cache_k_a.shape[1], cache_k_a.shape[2]
    n_heads = GROUP // HEAD_DIM
    n_pages = page_table.shape[1]
    past = n_pages * page
    n_exp = w_router.shape[-1]
    alpha = (2 * depth) ** 0.25
    assert past % MOBA_BLOCK == 0 and page == PAGE_ROWS and cache_k_a.shape[3] * cache_k_a.shape[4] == GROUP
    n_p = bsz * seq
    n_s = nseq * t_new
    tm = 512

    xp = x_prompt.reshape(n_p, d)
    xs = x_sample.reshape(n_s, d)
    cos_p, sin_p = _rope_tables(jnp.arange(seq))
    cos_s, sin_s = _rope_tables(jnp.tile(past + jnp.arange(t_new), nseq))
    page_t = lambda c: jnp.transpose(c, (0, 1, 3, 4, 2)).reshape(depth, n_pool, GROUP, page)
    kt_a, vt_a, kt_b, vt_b = page_t(cache_k_a), page_t(cache_v_a), page_t(cache_k_b), page_t(cache_v_b)
    lft_b = jnp.transpose(cache_logf_b, (0, 1, 3, 2))
    ones_p = jnp.ones((n_p, 1), F32)
    ones_s = jnp.ones((n_s, 1), F32)

    state = {"depth": depth, "bufs": None}
    outs_s = [[] for _ in range(5)]
    for l in range(depth):
        bfp = jnp.pad(b_f[l], (0, LANES - n_heads)).reshape(1, LANES)
        g0, b0 = ln_g[l, 0].reshape(1, d), ln_b[l, 0].reshape(1, d)
        g1, b1 = ln_g[l, 1].reshape(1, d), ln_b[l, 1].reshape(1, d)
        moe = l % 2 == 1
        wr = jnp.pad(w_router[l // 2], ((0, 0), (0, LANES - n_exp))) if moe else None

        wmain, wf = _split_w_in(w_in[l], BF16)
        qp, kvp, gates_p = _project_prompt(xp, wmain, wf, bfp, cos_p, sin_p, state, l, tm=tm, table_tiles=seq // tm)
        lfp = state["bufs"][4]
        cp = _cumsum(lfp.reshape(depth, bsz, seq, LANES), l)
        qp3 = qp.reshape(bsz, seq, 2 * GROUP)
        kvp3 = kvp.reshape(bsz, seq, 4 * GROUP)
        ya = _attn_prompt(qp3, kvp3, None, mode="moba", qoff=0, koff=0).reshape(n_p, GROUP)
        yb = _attn_prompt(qp3, kvp3, cp, mode="fox", qoff=1, koff=2).reshape(n_p, GROUP)
        wpa, wpb, wo = w_pa[l].astype(BF16), w_pb[l].astype(BF16), w_o[l].astype(BF16)
        res_p = _merge(ya, yb, gates_p, 0, 1, xp, wpa, wpb, wo,
                       g0, b0, wr, tm=tm, hp=False, alpha=alpha, n_exp=n_exp if moe else 0)

        ps, lfs = _project_sample(xs, wmain, wf, bfp, cos_s, sin_s)
        ps3 = ps.reshape(nseq, t_new, N_GROUPS * GROUP)
        ya_s = _moba_decode(kt_a, vt_a, page_table, l, ps3, qg=0, kg=1, vg=2)
        lfn = jnp.transpose(lfs[:, :n_heads].reshape(nseq, t_new, n_heads), (0, 2, 1))
        lfn = jnp.pad(lfn, ((0, 0), (0, 0), (0, LANES - t_new)))
        yb_s = _fox_decode(kt_b, vt_b, lft_b, page_table, l, ps3, lfn, qg=3, kg=4, vg=5)
        res_s = _merge(ya_s.reshape(n_s, GROUP), yb_s.reshape(n_s, GROUP), ps, 3, 4, xs, wpa, wpb, wo,
                       g0, b0, wr, tm=n_s, hp=False, alpha=alpha, n_exp=n_exp if moe else 0)

        if not moe:
            zero_te = jnp.zeros((n_p // tm,), jnp.int32)
            wg, wu, wd = (w[l // 2][None].astype(BF16) for w in (w_gate_d, w_up_d, w_down_d))
            xp = _ffn(res_p[0], ones_p, zero_te, jnp.full((1,), n_p // tm, jnp.int32), wg, wu, wd, g1, b1,
                      tm=tm, tf=512, hp=False, alpha=alpha, fuse_ln=True)
            xs = _ffn(res_s[0], ones_s, zero_te[:1], jnp.ones((1,), jnp.int32), wg, wu, wd, g1, b1,
                      tm=n_s, tf=512, hp=False, alpha=alpha, fuse_ln=True)
        else:
            x1 = jnp.concatenate([res_p[0], res_s[0]], axis=0)
            top_idx = jnp.concatenate([res_p[1][:, :TOP_K], res_s[1][:, :TOP_K]], axis=0)
            top_w = jnp.concatenate([res_p[2][:, :TOP_K], res_s[2][:, :TOP_K]], axis=0)
            row_token, row_gate, dest, tile_expert, n_live = _moe_plan(top_idx, top_w, n_exp, tm)
            xs_rows = jnp.take(x1, row_token, axis=0, mode="clip")
            ys = _ffn(xs_rows, row_gate[:, None], tile_expert, n_live,
                      w_gate_e[l // 2].astype(BF16), w_up_e[l // 2].astype(BF16), w_down_e[l // 2].astype(BF16),
                      g1, b1, tm=tm, tf=512, hp=False, alpha=alpha, fuse_ln=False)
            pick = lambda rows, k: jnp.take(ys, rows[:, k], axis=0, mode="clip")
            xp = _combine(res_p[0], pick(dest[:n_p], 0), pick(dest[:n_p], 1), g1, b1, tm=tm, alpha=alpha)
            xs = _combine(res_s[0], pick(dest[n_p:], 0), pick(dest[n_p:], 1), g1, b1, tm=n_s, alpha=alpha)

        for j, grp in enumerate((1, 2, 4, 5)):
            outs_s[j].append(ps[:, grp * GROUP:(grp + 1) * GROUP].reshape(nseq, t_new, n_heads, HEAD_DIM))
        outs_s[4].append(lfs[:, :n_heads].reshape(nseq, t_new, n_heads))

    ka, va, kb, vb, lfp = state["bufs"]
    heads = lambda a: jnp.transpose(a.reshape(depth, bsz, n_heads, HEAD_DIM, seq), (0, 1, 4, 2, 3))
    return (xp.reshape(bsz, seq, d), xs.reshape(nseq, t_new, d),
            heads(ka), heads(va), heads(kb), heads(vb), lfp[:, :, :n_heads].reshape(depth, bsz, seq, n_heads),
            *[jnp.stack(o) for o in outs_s])
```
